```python
import jax, jax.numpy as jnp
from jax import lax
import numpy as np

D_MODEL = 4096
BATCH = 2
SEQ = 8192
DEPTH = 4

RWKV_HEAD = 64
RWKV_W = D_MODEL // 2
RWKV_HEADS = RWKV_W // RWKV_HEAD
DECAY_LORA = 96
ICLR_LORA = 96
GATE_LORA = 256
RWKV_CHUNK = 16
CONV_W = D_MODEL // 4
CONV_K = 31
SB_HEAD = 128
SB_W = D_MODEL - RWKV_W - CONV_W
SB_HEADS = SB_W // SB_HEAD
SB_BLOCK = 128
RWKV_COLS = 3 * RWKV_W + DECAY_LORA + ICLR_LORA + GATE_LORA
PROJ_COLS = RWKV_COLS + 2 * CONV_W + 3 * SB_W
D_FF = 2 * D_MODEL
FFN_CONV_K = 3
NORM_EPS = 1e-6
LN_EPS = 1e-5
GN_EPS = RWKV_HEAD * 1e-5

kernel_name = 'hybrid_rwkv7_conformer_stickbreak_trunk'


def rms_norm(x, g):
    xf = x.astype(jnp.float32)
    y = xf * lax.rsqrt(jnp.mean(xf * xf, axis=-1, keepdims=True) + NORM_EPS)
    return (y * g.astype(jnp.float32)).astype(x.dtype)


def causal_dwconv(x, w):
    k = w.shape[0]
    return lax.conv_general_dilated(
        x, w[:, None, :], window_strides=(1,), padding=[(k - 1, 0)],
        dimension_numbers=('NWC', 'WIO', 'NWC'), feature_group_count=x.shape[-1])


def token_shift(p, mu):
    prev = jnp.pad(p, ((0, 0), (1, 0), (0, 0)))[:, :-1]
    return p + mu * (prev - p)


def rwkv7_time_mix(p, mu, w0, w_up, a0, a_up, g_up, k_k, k_a, r_k, gn_g, gn_b):
    f32 = jnp.float32
    bsz, seq, _ = p.shape
    C, N, H = RWKV_CHUNK, RWKV_HEAD, RWKV_HEADS
    nc = seq // C
    p = token_shift(p, mu)
    r, k, v, w_lo, a_lo, g_lo = jnp.split(
        p, [RWKV_W, 2 * RWKV_W, 3 * RWKV_W, 3 * RWKV_W + DECAY_LORA,
            3 * RWKV_W + DECAY_LORA + ICLR_LORA], axis=-1)
    log_w = (-jax.nn.softplus(-(w0 + jnp.tanh(w_lo) @ w_up)) - 0.5).astype(f32)
    log_decay = -jnp.exp(log_w)
    iclr = jax.nn.sigmoid((a0 + a_lo @ a_up).astype(f32))
    g = (jax.nn.sigmoid(g_lo) @ g_up).astype(f32)

    def heads(t):
        return t.astype(f32).reshape(bsz, seq, H, N)

    kk = heads(k * k_k)
    kk = kk * lax.rsqrt(jnp.maximum(jnp.sum(kk * kk, axis=-1, keepdims=True), 1e-24))
    kt = heads(k.astype(f32) * (1.0 + (iclr - 1.0) * k_a.astype(f32)))
    rh, vh, lwh, ah = heads(r), heads(v), heads(log_decay), heads(iclr)

    def chunks(t):
        return t.reshape(bsz, nc, C, H, N).transpose(0, 3, 1, 2, 4)

    rc, kc, vc, lwc = chunks(rh), chunks(kt), chunks(vh), chunks(lwh)
    ac, bc = chunks(-kk), chunks(kk * ah)
    cum = jnp.cumsum(lwc, axis=3)
    tot = cum[:, :, :, -1:, :]
    a_t = ac * jnp.exp(cum - lwc)
    r_t = rc * jnp.exp(cum)
    inv = jnp.exp(-cum)
    b_hat, k_hat = bc * inv, kc * inv
    to_end = jnp.exp(tot - cum)
    b_end, k_end = bc * to_end, kc * to_end
    strict = jnp.asarray(np.tril(np.ones((C, C), np.float32), -1))
    incl = jnp.asarray(np.tril(np.ones((C, C), np.float32)))
    m_ab = jnp.einsum('bhntk,bhnik->bhnti', a_t, b_hat) * strict
    m_ak = jnp.einsum('bhntk,bhnik->bhnti', a_t, k_hat) * strict
    q_b = jnp.einsum('bhntk,bhnik->bhnti', r_t, b_hat) * incl
    q_k = jnp.einsum('bhntk,bhnik->bhnti', r_t, k_hat) * incl
    rhs = jnp.concatenate([a_t, jnp.einsum('bhnti,bhniv->bhntv', m_ak, vc)], axis=-1)
    sol = lax.linalg.triangular_solve(jnp.eye(C, dtype=f32) - m_ab, rhs, left_side=True,
                                      lower=True, unit_diagonal=True)
    wa, u0 = jnp.split(sol, [N], axis=-1)
    rq = r_t + jnp.einsum('bhnti,bhnik->bhntk', q_b, wa)
    o_in = (jnp.einsum('bhnti,bhniv->bhntv', q_b, u0)
            + jnp.einsum('bhnti,bhniv->bhntv', q_k, vc))
    trans = (jnp.einsum('bhnik,bhnij->bhnkj', wa, b_end)
             + jnp.exp(tot[:, :, :, 0, :])[..., None] * jnp.eye(N, dtype=f32))
    delta = (jnp.einsum('bhniv,bhnik->bhnvk', u0, b_end)
             + jnp.einsum('bhniv,bhnik->bhnvk', vc, k_end))

    def step(state, inp):
        rq_n, oin_n, tr_n, d_n = inp
        o = jnp.einsum('bhtk,bhvk->bhtv', rq_n, state) + oin_n
        state = jnp.einsum('bhvk,bhkj->bhvj', state, tr_n) + d_n
        return state, o

    xs = tuple(jnp.moveaxis(t, 2, 0) for t in (rq, o_in, trans, delta))
    s0 = jnp.zeros((bsz, H, N, N), f32)
    _, o = lax.scan(step, s0, xs)
    o = o.transpose(1, 0, 3, 2, 4).reshape(bsz, seq, H, N)
    mean = jnp.mean(o, axis=-1, keepdims=True)
    var = jnp.mean(jnp.square(o - mean), axis=-1, keepdims=True)
    o = ((o - mean) * lax.rsqrt(var + GN_EPS)).reshape(bsz, seq, RWKV_W)
    o = o * gn_g.astype(f32) + gn_b.astype(f32)
    bonus = jnp.sum(rh * kt * r_k.astype(f32), axis=-1, keepdims=True) * vh
    o = o + bonus.reshape(bsz, seq, RWKV_W)
    return (o * g).astype(p.dtype)


def conformer_conv(p, conv_w, conv_b, ln_g, ln_b):
    f32 = jnp.float32
    val, gate = jnp.split(p, 2, axis=-1)
    u = causal_dwconv(val * jax.nn.sigmoid(gate), conv_w) + conv_b
    uf = u.astype(f32)
    mean = jnp.mean(uf, axis=-1, keepdims=True)
    var = jnp.mean(jnp.square(uf - mean), axis=-1, keepdims=True)
    y = (uf - mean) * lax.rsqrt(var + LN_EPS) * ln_g.astype(f32) + ln_b.astype(f32)
    return jax.nn.silu(y).astype(p.dtype)


def stick_breaking_attn(p, q_g, k_g):
    f32 = jnp.float32
    bsz, seq, _ = p.shape
    BLK = SB_BLOCK
    q, k, v = jnp.split(p, 3, axis=-1)

    def heads(t):
        return t.reshape(bsz, seq, SB_HEADS, SB_HEAD)

    q = rms_norm(heads(q), q_g).astype(f32).transpose(0, 2, 1, 3) * (SB_HEAD ** -0.5)
    k = rms_norm(heads(k), k_g).astype(f32).transpose(0, 2, 1, 3)
    v = heads(v).astype(f32).transpose(0, 2, 1, 3)
    nb = seq // BLK
    after_in = jnp.asarray(np.tril(np.ones((BLK, BLK), np.float32), -1))
    after_blk = jnp.asarray(np.tril(np.ones((nb, nb), np.float32), -1))
    outs = []
    for b in range(nb):
        nk = b + 1
        kl = nk * BLK
        z = jnp.einsum('bhqd,bhkd->bhqk', q[:, :, b * BLK:(b + 1) * BLK], k[:, :, :kl])
        mask = jnp.asarray(np.arange(kl)[None, :] < (b * BLK + np.arange(BLK))[:, None])
        l = jnp.where(mask, jax.nn.log_sigmoid(-z), 0.0).reshape(bsz, SB_HEADS, BLK, nk, BLK)
        after = (jnp.einsum('bhqnj,js->bhqns', l, after_in)
                 + jnp.einsum('bhqm,mn->bhqn', jnp.sum(l, axis=-1), after_blk[:nk, :nk])[..., None])
        att = jnp.where(mask, jnp.exp(jax.nn.log_sigmoid(z)
                                      + after.reshape(bsz, SB_HEADS, BLK, kl)), 0.0)
        outs.append(jnp.einsum('bhqk,bhkd->bhqd', att, v[:, :, :kl]))
    o = jnp.concatenate(outs, axis=2)
    o = o.transpose(0, 2, 1, 3).reshape(bsz, seq, SB_W)
    return o.astype(p.dtype)


def setup_inputs(seed: int = 0) -> dict:
    key = jax.random.key(seed)
    ks = jax.random.split(key, 26)
    f32 = jnp.float32
    L = DEPTH

    def nrm(k, shape, s):
        return jax.random.normal(k, shape, f32) * s

    return {
        'x': nrm(ks[0], (BATCH, SEQ, D_MODEL), 1.0),
        'w_in': nrm(ks[1], (L, D_MODEL, PROJ_COLS), D_MODEL ** -0.5),
        'w_out': nrm(ks[2], (L, D_MODEL, D_MODEL), D_MODEL ** -0.5),
        'attn_norm': 1.0 + nrm(ks[3], (L, D_MODEL), 0.1),
        'ffn_norm': 1.0 + nrm(ks[4], (L, D_MODEL), 0.1),
        'rwkv_mu': jax.random.uniform(ks[5], (L, RWKV_COLS), f32),
        'rwkv_w0': nrm(ks[6], (L, RWKV_W), 0.5),
        'rwkv_w_up': nrm(ks[7], (L, DECAY_LORA, RWKV_W), 0.1),
        'rwkv_a0': nrm(ks[8], (L, RWKV_W), 0.5),
        'rwkv_a_up': nrm(ks[9], (L, ICLR_LORA, RWKV_W), 0.5 * ICLR_LORA ** -0.5),
        'rwkv_g_up': nrm(ks[10], (L, GATE_LORA, RWKV_W), GATE_LORA ** -0.5),
        'rwkv_k_k': 0.85 + nrm(ks[11], (L, RWKV_W), 0.1),
        'rwkv_k_a': 1.0 + nrm(ks[12], (L, RWKV_W), 0.1),
        'rwkv_r_k': nrm(ks[13], (L, RWKV_HEADS, RWKV_HEAD), 0.1),
        'rwkv_gn_g': 1.0 + nrm(ks[14], (L, RWKV_W), 0.1),
        'rwkv_gn_b': nrm(ks[15], (L, RWKV_W), 0.01),
        'conv_w': nrm(ks[16], (L, CONV_K, CONV_W), CONV_K ** -0.5),
        'conv_b': nrm(ks[17], (L, CONV_W), 0.01),
        'conv_ln_g': 1.0 + nrm(ks[18], (L, CONV_W), 0.1),
        'conv_ln_b': nrm(ks[19], (L, CONV_W), 0.01),
        'sb_q_norm': 1.0 + nrm(ks[20], (L, SB_HEAD), 0.1),
        'sb_k_norm': 1.0 + nrm(ks[21], (L, SB_HEAD), 0.1),
        'ffn_up': nrm(ks[22], (L, D_MODEL, 2 * D_FF), D_MODEL ** -0.5),
        'ffn_conv': nrm(ks[23], (L, FFN_CONV_K, 2 * D_FF), FFN_CONV_K ** -0.5),
        'ffn_down': nrm(ks[24], (L, D_FF, D_MODEL), D_FF ** -0.5),
    }


def reference(x, w_in, w_out, attn_norm, ffn_norm, rwkv_mu, rwkv_w0, rwkv_w_up, rwkv_a0,
              rwkv_a_up, rwkv_g_up, rwkv_k_k, rwkv_k_a, rwkv_r_k, rwkv_gn_g, rwkv_gn_b,
              conv_w, conv_b, conv_ln_g, conv_ln_b, sb_q_norm, sb_k_norm,
              ffn_up, ffn_conv, ffn_down):
    for l in range(DEPTH):
        h = rms_norm(x, attn_norm[l])
        p = h @ w_in[l]
        p_a, p_b, p_c = jnp.split(p, [RWKV_COLS, RWKV_COLS + 2 * CONV_W], axis=-1)
        y_a = rwkv7_time_mix(p_a, rwkv_mu[l], rwkv_w0[l], rwkv_w_up[l], rwkv_a0[l],
                             rwkv_a_up[l], rwkv_g_up[l], rwkv_k_k[l], rwkv_k_a[l],
                             rwkv_r_k[l], rwkv_gn_g[l], rwkv_gn_b[l])
        y_b = conformer_conv(p_b, conv_w[l], conv_b[l], conv_ln_g[l], conv_ln_b[l])
        y_c = stick_breaking_attn(p_c, sb_q_norm[l], sb_k_norm[l])
        y = jnp.concatenate([y_a, y_b, y_c], axis=-1)
        x = x + y @ w_out[l]
        h = rms_norm(x, ffn_norm[l])
        u = causal_dwconv(h @ ffn_up[l], ffn_conv[l])
        gate, val = jnp.split(u, 2, axis=-1)
        x = x + (jax.nn.silu(gate) * val) @ ffn_down[l]
    return x
```

```python
import functools

import jax
import jax.numpy as jnp
from jax import lax
from jax.experimental import pallas as pl
from jax.experimental.pallas import tpu as pltpu

F32 = jnp.float32
BF16 = jnp.bfloat16

D_MODEL = 4096
DEPTH = 4
RWKV_HEAD = 64
RWKV_W = 2048
RWKV_HEADS = RWKV_W // RWKV_HEAD
DECAY_LORA = 96
ICLR_LORA = 96
GATE_LORA = 256
CONV_W = 1024
CONV_K = 31
SB_HEAD = 128
SB_W = 1024
SB_HEADS = SB_W // SB_HEAD
D_FF = 2 * D_MODEL
FFN_CONV_K = 3
NORM_EPS = 1e-6
LN_EPS = 1e-5
GN_EPS = RWKV_HEAD * 1e-5

LANES = 128
LORA_PAD = 128
COL_RKV = 0
COL_CVAL = 3 * RWKV_W
COL_CGATE = COL_CVAL + CONV_W
COL_SBQ = COL_CGATE + CONV_W
COL_SBK = COL_SBQ + SB_W
COL_SBV = COL_SBK + SB_W
COL_LORA = COL_SBV + SB_W
LORA_COLS = 2 * LORA_PAD + GATE_LORA
PROJ_PAD = COL_LORA + LORA_COLS
VMEM_LIMIT = 56 * 1024 * 1024

RWKV_C = 64
PAIR = 2 * RWKV_HEAD


def _cparams(sem):
    return pltpu.CompilerParams(dimension_semantics=sem, vmem_limit_bytes=VMEM_LIMIT)


def _dot(a, b):
    return jnp.dot(a.astype(BF16), b.astype(BF16), preferred_element_type=F32)


def _dot_nt(a, b):
    return lax.dot_general(a.astype(BF16), b.astype(BF16), (((1,), (1,)), ((), ())),
                           preferred_element_type=F32)


def _dot_tn(a, b):
    return lax.dot_general(a.astype(BF16), b.astype(BF16), (((0,), (0,)), ((), ())),
                           preferred_element_type=F32)


def _split3(x):
    hi = x.astype(BF16)
    r1 = x - hi.astype(F32)
    mid = r1.astype(BF16)
    lo = (r1 - mid.astype(F32)).astype(BF16)
    return hi, mid, lo


def _dot_x3(a, b_exact):
    hi, mid, lo = _split3(a)
    b = b_exact.astype(BF16)
    return (jnp.dot(hi, b, preferred_element_type=F32)
            + jnp.dot(mid, b, preferred_element_type=F32)
            + jnp.dot(lo, b, preferred_element_type=F32))


def _x3_dot(a_exact, b):
    hi, mid, lo = _split3(b)
    a = a_exact.astype(BF16)
    return (jnp.dot(a, hi, preferred_element_type=F32)
            + jnp.dot(a, mid, preferred_element_type=F32)
            + jnp.dot(a, lo, preferred_element_type=F32))


def _dot_hi(a, b):
    ah = a.astype(BF16)
    al = (a - ah.astype(F32)).astype(BF16)
    bh = b.astype(BF16)
    bl = (b - bh.astype(F32)).astype(BF16)
    return (jnp.dot(ah, bh, preferred_element_type=F32)
            + jnp.dot(al, bh, preferred_element_type=F32)
            + jnp.dot(ah, bl, preferred_element_type=F32))


def _softplus(x):
    return jnp.maximum(x, 0.0) + jnp.log1p(jnp.exp(-jnp.abs(x)))


def _rmsnorm_kernel(x_ref, g_ref, o_ref):
    x = x_ref[...]
    ms = jnp.mean(x * x, axis=-1, keepdims=True)
    o_ref[...] = (x * lax.rsqrt(ms + NORM_EPS) * g_ref[...]).astype(o_ref.dtype)


def _rmsnorm(x, g, tm=256):
    m, d = x.shape
    return pl.pallas_call(
        _rmsnorm_kernel,
        grid=(m // tm,),
        in_specs=[pl.BlockSpec((tm, d), lambda i: (i, 0)),
                  pl.BlockSpec((1, d), lambda i: (0, 0))],
        out_specs=pl.BlockSpec((tm, d), lambda i: (i, 0)),
        out_shape=jax.ShapeDtypeStruct((m, d), BF16),
        compiler_params=_cparams(("parallel",)),
        name="rmsnorm",
    )(x, g.reshape(1, d))


def _mm_kernel(*refs, n_lhs, has_res):
    o_ref = refs[-1]
    acc = None
    for a_ref, w_ref in zip(refs[:n_lhs], refs[n_lhs:2 * n_lhs]):
        d = jnp.dot(a_ref[...], w_ref[...], preferred_element_type=F32)
        acc = d if acc is None else acc + d
    if has_res:
        acc = acc + refs[2 * n_lhs][...]
    o_ref[...] = acc.astype(o_ref.dtype)


def _matmul(lhs_list, w, residual=None, tm=1024, tn=512, name="matmul"):
    m = lhs_list[0].shape[0]
    n = w.shape[1]
    in_specs, args = [], []
    for a in lhs_list:
        in_specs.append(pl.BlockSpec((tm, a.shape[1]), lambda i, j: (i, 0)))
        args.append(a)
    row = 0
    for a in lhs_list:
        k = a.shape[1]
        assert row % k == 0
        in_specs.append(pl.BlockSpec((k, tn), functools.partial(lambda i, j, rb: (rb, j), rb=row // k)))
        args.append(w)
        row += k
    assert row == w.shape[0]
    if residual is not None:
        in_specs.append(pl.BlockSpec((tm, tn), lambda i, j: (i, j)))
        args.append(residual)
    return pl.pallas_call(
        functools.partial(_mm_kernel, n_lhs=len(lhs_list), has_res=residual is not None),
        grid=(m // tm, n // tn),
        in_specs=in_specs,
        out_specs=pl.BlockSpec((tm, tn), lambda i, j: (i, j)),
        out_shape=jax.ShapeDtypeStruct((m, n), F32),
        compiler_params=_cparams(("parallel", "arbitrary")),
        name=name,
    )(*args)


FFN_HALO = 16


def _ffn_up_kernel(h_ref, halo_ref, wg_ref, wv_ref, cg_ref, cv_ref, o_ref, sg, sv, *, tm, seq):
    i = pl.program_id(0)
    first = (i * tm) % seq == 0
    h = h_ref[...]
    halo = jnp.where(first, jnp.zeros_like(halo_ref[...]), halo_ref[...])

    def conv(w_ref, c_ref, s):
        s[0:FFN_HALO, :] = jnp.dot(halo, w_ref[...], preferred_element_type=F32)
        s[FFN_HALO:FFN_HALO + tm, :] = jnp.dot(h, w_ref[...], preferred_element_type=F32)
        acc = None
        for k in range(FFN_CONV_K):
            off = FFN_HALO - (FFN_CONV_K - 1) + k
            term = c_ref[k:k + 1, :] * s[off:off + tm, :]
            acc = term if acc is None else acc + term
        return acc

    g = conv(wg_ref, cg_ref, sg)
    v = conv(wv_ref, cv_ref, sv)
    o_ref[...] = (g * jax.nn.sigmoid(g) * v).astype(o_ref.dtype)


def _ffn_up(h, w_up, conv, seq, tm=1024, tn=512):
    m, d = h.shape
    nj = D_FF // tn
    return pl.pallas_call(
        functools.partial(_ffn_up_kernel, tm=tm, seq=seq),
        grid=(m // tm, nj),
        in_specs=[
            pl.BlockSpec((tm, d), lambda i, j: (i, 0)),
            pl.BlockSpec((FFN_HALO, d), lambda i, j: (jnp.maximum(i * (tm // FFN_HALO) - 1, 0), 0)),
            pl.BlockSpec((d, tn), lambda i, j: (0, j)),
            pl.BlockSpec((d, tn), lambda i, j: (0, j + nj)),
            pl.BlockSpec((FFN_CONV_K, tn), lambda i, j: (0, j)),
            pl.BlockSpec((FFN_CONV_K, tn), lambda i, j: (0, j + nj)),
        ],
        out_specs=pl.BlockSpec((tm, tn), lambda i, j: (i, j)),
        out_shape=jax.ShapeDtypeStruct((m, D_FF), BF16),
        scratch_shapes=[pltpu.VMEM((FFN_HALO + tm, tn), F32),
                        pltpu.VMEM((FFN_HALO + tm, tn), F32)],
        compiler_params=_cparams(("parallel", "arbitrary")),
        name="ffn_up_conv_gate",
    )(h, h, w_up, w_up, conv, conv)


CONF_HALO = 32
CONF_ROWS = 32
CONF_COLS = 256


def _conformer_kernel(val_ref, gate_ref, hval_ref, hgate_ref, cw_ref, cb_ref, lng_ref, lnb_ref,
                      o_ref, glu_s, u_s, *, tm, seq):
    i = pl.program_id(0)
    first = (i * tm) % seq == 0
    hglu = hval_ref[...] * jax.nn.sigmoid(hgate_ref[...])
    glu_s[0:CONF_HALO, :] = jnp.where(first, jnp.zeros_like(hglu), hglu)
    glu_s[CONF_HALO:CONF_HALO + tm, :] = val_ref[...] * jax.nn.sigmoid(gate_ref[...])
    base = CONF_HALO - (CONV_K - 1)
    for r0 in range(0, tm, CONF_ROWS):
        for c0 in range(0, CONV_W, CONF_COLS):
            acc = jnp.zeros((CONF_ROWS, CONF_COLS), F32) + cb_ref[:, c0:c0 + CONF_COLS]
            for k in range(CONV_K):
                acc = acc + cw_ref[k:k + 1, c0:c0 + CONF_COLS] * glu_s[r0 + base + k:r0 + base + k + CONF_ROWS,
                                                                   c0:c0 + CONF_COLS]
            u_s[r0:r0 + CONF_ROWS, c0:c0 + CONF_COLS] = acc
    u = u_s[...]
    mean = jnp.mean(u, axis=-1, keepdims=True)
    d = u - mean
    var = jnp.mean(d * d, axis=-1, keepdims=True)
    y = d * lax.rsqrt(var + LN_EPS) * lng_ref[...] + lnb_ref[...]
    o_ref[...] = (y * jax.nn.sigmoid(y)).astype(o_ref.dtype)


def _conformer(p, cw, cb, lng, lnb, seq, tm=256):
    m = p.shape[0]
    vb, gb = COL_CVAL // CONV_W, COL_CGATE // CONV_W
    hmap = lambda i: jnp.maximum(i * (tm // CONF_HALO) - 1, 0)
    return pl.pallas_call(
        functools.partial(_conformer_kernel, tm=tm, seq=seq),
        grid=(m // tm,),
        in_specs=[
            pl.BlockSpec((tm, CONV_W), lambda i: (i, vb)),
            pl.BlockSpec((tm, CONV_W), lambda i: (i, gb)),
            pl.BlockSpec((CONF_HALO, CONV_W), lambda i: (hmap(i), vb)),
            pl.BlockSpec((CONF_HALO, CONV_W), lambda i: (hmap(i), gb)),
            pl.BlockSpec((CONV_K, CONV_W), lambda i: (0, 0)),
            pl.BlockSpec((1, CONV_W), lambda i: (0, 0)),
            pl.BlockSpec((1, CONV_W), lambda i: (0, 0)),
            pl.BlockSpec((1, CONV_W), lambda i: (0, 0)),
        ],
        out_specs=pl.BlockSpec((tm, CONV_W), lambda i: (i, 0)),
        out_shape=jax.ShapeDtypeStruct((m, CONV_W), BF16),
        scratch_shapes=[pltpu.VMEM((CONF_HALO + tm, CONV_W), F32),
                        pltpu.VMEM((tm, CONV_W), F32)],
        compiler_params=_cparams(("parallel",)),
        name="conformer_conv",
    )(p, p, p, p, cw, cb.reshape(1, -1), lng.reshape(1, -1), lnb.reshape(1, -1))


SB_BQ = 256
SB_BK = 256
SB_PREP_ROWS = 512


def _sb_kernel(q_ref, k_ref, v_ref, qg_ref, kg_ref, o_ref, kn_s, vb_s, *, seq):
    qi = pl.program_id(2)

    @pl.when(qi == 0)
    def _():
        def body(c, carry):
            sl = pl.ds(pl.multiple_of(c * SB_PREP_ROWS, SB_PREP_ROWS), SB_PREP_ROWS)
            k = k_ref[sl, :]
            ms = jnp.mean(k * k, axis=-1, keepdims=True)
            kn_s[sl, :] = (k * lax.rsqrt(ms + NORM_EPS) * kg_ref[...]).astype(BF16)
            vb_s[sl, :] = v_ref[sl, :].astype(BF16)
            return carry
        lax.fori_loop(0, seq // SB_PREP_ROWS, body, 0)

    q = q_ref[...]
    ms = jnp.mean(q * q, axis=-1, keepdims=True)
    q = (q * lax.rsqrt(ms + NORM_EPS) * qg_ref[...] * (SB_HEAD ** -0.5)).astype(BF16)

    row = lax.broadcasted_iota(jnp.int32, (SB_BK, SB_BK), 0)
    col = lax.broadcasted_iota(jnp.int32, (SB_BK, SB_BK), 1)
    later = (row > col).astype(BF16)
    causal = col < row

    def block(kb, c, acc, masked):
        sl = pl.ds(pl.multiple_of(kb * SB_BK, SB_BK), SB_BK)
        z = lax.dot_general(q, kn_s[sl, :], (((1,), (1,)), ((), ())), preferred_element_type=F32)
        sp = _softplus(z)
        l = -sp
        if masked:
            l = jnp.where(causal, l, 0.0)
        l_hi = l.astype(BF16)
        l_lo = (l - l_hi.astype(F32)).astype(BF16)
        after = (jnp.dot(l_hi, later, preferred_element_type=F32)
                 + jnp.dot(l_lo, later, preferred_element_type=F32))
        att = jnp.exp((z - sp) + after + c)
        if masked:
            att = jnp.where(causal, att, 0.0)
        acc = acc + jnp.dot(att.astype(BF16), vb_s[sl, :], preferred_element_type=F32)
        c = c + jnp.sum(l, axis=-1, keepdims=True)
        return c, acc

    c0 = jnp.zeros((SB_BQ, 1), F32)
    acc0 = jnp.zeros((SB_BQ, SB_HEAD), F32)
    c, acc = block(qi, c0, acc0, True)

    def body(n, carry):
        return block(qi - 1 - n, carry[0], carry[1], False)

    c, acc = lax.fori_loop(0, qi, body, (c, acc))
    o_ref[...] = acc.astype(o_ref.dtype)


def _stick_breaking(p, qg, kg, bsz, seq):
    assert SB_BQ == SB_BK
    nq = seq // SB_BQ
    qb, kb, vb = COL_SBQ // SB_HEAD, COL_SBK // SB_HEAD, COL_SBV // SB_HEAD
    return pl.pallas_call(
        functools.partial(_sb_kernel, seq=seq),
        grid=(bsz, SB_HEADS, nq),
        in_specs=[
            pl.BlockSpec((SB_BQ, SB_HEAD), lambda b, h, q: (b * nq + q, qb + h)),
            pl.BlockSpec((seq, SB_HEAD), lambda b, h, q: (b, kb + h)),
            pl.BlockSpec((seq, SB_HEAD), lambda b, h, q: (b, vb + h)),
            pl.BlockSpec((1, SB_HEAD), lambda b, h, q: (0, 0)),
            pl.BlockSpec((1, SB_HEAD), lambda b, h, q: (0, 0)),
        ],
        out_specs=pl.BlockSpec((SB_BQ, SB_HEAD), lambda b, h, q: (b * nq + q, h)),
        out_shape=jax.ShapeDtypeStruct((bsz * seq, SB_W), BF16),
        scratch_shapes=[pltpu.VMEM((seq, SB_HEAD), BF16), pltpu.VMEM((seq, SB_HEAD), BF16)],
        compiler_params=_cparams(("parallel", "parallel", "arbitrary")),
        name="stick_breaking",
    )(p, p, p, qg.reshape(1, -1), kg.reshape(1, -1))


PREP_HALO = 8


def _head_ones():
    r = lax.broadcasted_iota(jnp.int32, (PAIR, PAIR), 0) // RWKV_HEAD
    c = lax.broadcasted_iota(jnp.int32, (PAIR, PAIR), 1) // RWKV_HEAD
    return (r == c).astype(BF16)


def _rwkv_prep_kernel(p_ref, lora_ref, hp_ref, hlora_ref, mu_ref, mul_ref, w0_ref, wup_ref, a0_ref,
                      aup_ref, gup_ref, kk_ref, ka_ref, rk_ref,
                      r_o, k_o, v_o, lw_o, a_o, b_o, bonus_o, g_o, *, tm, seq):
    i = pl.program_id(0)
    first = (i * tm) % seq == 0

    def shifted(x_ref, h_ref, m_ref, c0, c1):
        x = x_ref[:, c0:c1]
        prev = pltpu.roll(x, 1, 0)
        row = lax.broadcasted_iota(jnp.int32, x.shape, 0)
        last = h_ref[PREP_HALO - 1:PREP_HALO, c0:c1]
        last = jnp.where(first, jnp.zeros_like(last), last)
        prev = jnp.where(row == 0, last, prev)
        return x + m_ref[:, c0:c1] * (prev - x)

    lo = shifted(lora_ref, hlora_ref, mul_ref, 0, LORA_COLS)
    w_lo = lo[:, 0:LORA_PAD]
    a_lo = lo[:, LORA_PAD:2 * LORA_PAD]
    g_lo = lo[:, 2 * LORA_PAD:]
    ones = _head_ones()

    for c0 in range(0, RWKV_W, 512):
        c1 = c0 + 512
        r = shifted(p_ref, hp_ref, mu_ref, c0, c1)
        k = shifted(p_ref, hp_ref, mu_ref, RWKV_W + c0, RWKV_W + c1)
        v = shifted(p_ref, hp_ref, mu_ref, 2 * RWKV_W + c0, 2 * RWKV_W + c1)
        y = w0_ref[:, c0:c1] + _dot(jnp.tanh(w_lo), wup_ref[:, c0:c1])
        log_w = -_softplus(-y) - 0.5
        lw_o[:, c0:c1] = -jnp.exp(log_w)
        iclr = jax.nn.sigmoid(a0_ref[:, c0:c1] + _dot(a_lo, aup_ref[:, c0:c1]))
        g_o[:, c0:c1] = _dot(jax.nn.sigmoid(g_lo), gup_ref[:, c0:c1])
        kk = k * kk_ref[:, c0:c1]
        kt = k * (1.0 + (iclr - 1.0) * ka_ref[:, c0:c1])
        rkr = r * kt * rk_ref[:, c0:c1]
        for t0 in range(0, 512, PAIR):
            t1 = t0 + PAIR
            kk_t = kk[:, t0:t1]
            ss = _dot_x3(kk_t * kk_t, ones)
            kk_n = kk_t * lax.rsqrt(jnp.maximum(ss, 1e-24))
            a_o[:, c0 + t0:c0 + t1] = -kk_n
            b_o[:, c0 + t0:c0 + t1] = kk_n * iclr[:, t0:t1]
            bonus_o[:, c0 + t0:c0 + t1] = _dot_x3(rkr[:, t0:t1], ones) * v[:, t0:t1]
        r_o[:, c0:c1] = r
        k_o[:, c0:c1] = kt
        v_o[:, c0:c1] = v


def _rwkv_prep(p, mu, mul, w0, wup, a0, aup, gup, kk, ka, rk, seq, tm=128):
    m = p.shape[0]
    lb = COL_LORA // LORA_COLS
    hmap = lambda i: jnp.maximum(i * (tm // PREP_HALO) - 1, 0)
    row = lambda a: a.reshape(1, -1)
    vec = pl.BlockSpec((1, RWKV_W), lambda i: (0, 0))
    out = jax.ShapeDtypeStruct((m, RWKV_W), F32)
    ospec = pl.BlockSpec((tm, RWKV_W), lambda i: (i, 0))
    return pl.pallas_call(
        functools.partial(_rwkv_prep_kernel, tm=tm, seq=seq),
        grid=(m // tm,),
        in_specs=[
            pl.BlockSpec((tm, 3 * RWKV_W), lambda i: (i, 0)),
            pl.BlockSpec((tm, LORA_COLS), lambda i: (i, lb)),
            pl.BlockSpec((PREP_HALO, 3 * RWKV_W), lambda i: (hmap(i), 0)),
            pl.BlockSpec((PREP_HALO, LORA_COLS), lambda i: (hmap(i), lb)),
            pl.BlockSpec((1, 3 * RWKV_W), lambda i: (0, 0)),
            pl.BlockSpec((1, LORA_COLS), lambda i: (0, 0)),
            vec,
            pl.BlockSpec((LORA_PAD, RWKV_W), lambda i: (0, 0)),
            vec,
            pl.BlockSpec((LORA_PAD, RWKV_W), lambda i: (0, 0)),
            pl.BlockSpec((GATE_LORA, RWKV_W), lambda i: (0, 0)),
            vec, vec, vec,
        ],
        out_specs=[ospec] * 8,
        out_shape=[out] * 8,
        compiler_params=_cparams(("parallel",)),
        name="rwkv_prep",
    )(p, p, p, p, row(mu), row(mul), row(w0), wup, row(a0), aup, gup, row(kk), row(ka), row(rk))


def _rwkv_scan_kernel(r_ref, k_ref, v_ref, lw_ref, a_ref, b_ref, bonus_ref, g_ref, gng_ref, gnb_ref,
                      o_ref, s_ref, rq_s, oin_s, tr_s, dl_s, *, tblk):
    C = RWKV_C
    t_id = pl.program_id(2)

    @pl.when(t_id == 0)
    def _():
        s_ref[...] = jnp.zeros_like(s_ref)

    lane = lax.broadcasted_iota(jnp.int32, (1, PAIR), 1)
    m0 = (lane < RWKV_HEAD).astype(F32)
    m1 = 1.0 - m0
    ti = lax.broadcasted_iota(jnp.int32, (C, C), 0)
    ii = lax.broadcasted_iota(jnp.int32, (C, C), 1)
    tri_incl = (ii <= ti).astype(BF16)
    t2 = lax.broadcasted_iota(jnp.int32, (C, 2 * C), 0)
    i2 = lax.broadcasted_iota(jnp.int32, (C, 2 * C), 1) % C
    strict2 = (i2 < t2).astype(F32)
    incl2 = (i2 <= t2).astype(F32)
    rr = lax.broadcasted_iota(jnp.int32, (PAIR, PAIR), 0)
    cc = lax.broadcasted_iota(jnp.int32, (PAIR, PAIR), 1)
    bdmask = (rr // RWKV_HEAD == cc // RWKV_HEAD).astype(F32)
    eye = (rr == cc).astype(F32)
    ones = _head_ones()

    def stack_heads(x):
        return jnp.concatenate([x * m0, x * m1], axis=0)

    def chunk(c, carry):
        sl = pl.ds(pl.multiple_of(c * C, C), C)
        lw = lw_ref[sl, :]
        r, k, v, a, b = r_ref[sl, :], k_ref[sl, :], v_ref[sl, :], a_ref[sl, :], b_ref[sl, :]
        cum = _x3_dot(tri_incl, lw)
        tot = cum[C - 1:C, :]
        e_neg = jnp.exp(-cum)
        e_end = jnp.exp(tot - cum)
        at = a * jnp.exp(cum - lw)
        rt = r * jnp.exp(cum)
        bh, kh = b * e_neg, k * e_neg
        bend, kend = b * e_end, k * e_end
        gram = _dot_nt(jnp.concatenate([at, rt], axis=0),
                       jnp.concatenate([stack_heads(bh), stack_heads(kh)], axis=0))
        m_ab = gram[0:C, 0:2 * C] * strict2
        m_ak = gram[0:C, 2 * C:] * strict2
        q_b = gram[C:, 0:2 * C] * incl2
        q_k = gram[C:, 2 * C:] * incl2
        vv = stack_heads(v)
        u_rhs = _dot(m_ak, vv)
        bd = stack_heads(m_ab)
        t_inv = eye + bd
        pw = bd
        for _ in range(5):
            pw = _dot_hi(pw, pw)
            t_inv = t_inv + _dot_hi(t_inv, pw)
        t_ls = t_inv[0:C, :] + t_inv[C:, :]
        sol = _dot_hi(t_ls, jnp.concatenate(
            [jnp.concatenate([at * m0, u_rhs * m0], axis=1),
             jnp.concatenate([at * m1, u_rhs * m1], axis=1)], axis=0))
        wa, u0 = sol[:, 0:PAIR], sol[:, PAIR:]
        y = _dot(q_b, jnp.concatenate(
            [jnp.concatenate([wa * m0, u0 * m0], axis=1),
             jnp.concatenate([wa * m1, u0 * m1], axis=1)], axis=0))
        rq_s[sl, :] = rt + y[:, 0:PAIR]
        oin_s[sl, :] = y[:, PAIR:] + _dot(q_k, vv)
        tr_s[c] = _dot_tn(wa, bend) * bdmask + eye * jnp.exp(tot)
        dl_s[c] = (_dot_tn(u0, bend) + _dot_tn(v, kend)) * bdmask
        return carry

    lax.fori_loop(0, tblk // C, chunk, 0)

    def scan(c, carry):
        sl = pl.ds(pl.multiple_of(c * C, C), C)
        s = s_ref[...]
        o = _dot_nt(rq_s[sl, :], s) + oin_s[sl, :]
        s_ref[...] = _dot(s, tr_s[c]) + dl_s[c]
        mean = _dot_x3(o, ones) * (1.0 / RWKV_HEAD)
        d = o - mean
        var = _dot_x3(d * d, ones) * (1.0 / RWKV_HEAD)
        on = d * lax.rsqrt(var + GN_EPS) * gng_ref[...] + gnb_ref[...]
        o_ref[sl, :] = ((on + bonus_ref[sl, :]) * g_ref[sl, :]).astype(o_ref.dtype)
        return carry

    lax.fori_loop(0, tblk // C, scan, 0)


def _rwkv_scan(r, k, v, lw, a, b, bonus, g, gng, gnb, bsz, seq, tblk=512):
    nt = seq // tblk
    npair = RWKV_W // PAIR
    tile = pl.BlockSpec((tblk, PAIR), lambda bi, h, t: (bi * nt + t, h))
    vec = pl.BlockSpec((1, PAIR), lambda bi, h, t: (0, h))
    nch = tblk // RWKV_C
    return pl.pallas_call(
        functools.partial(_rwkv_scan_kernel, tblk=tblk),
        grid=(bsz, npair, nt),
        in_specs=[tile] * 8 + [vec, vec],
        out_specs=tile,
        out_shape=jax.ShapeDtypeStruct((bsz * seq, RWKV_W), BF16),
        scratch_shapes=[pltpu.VMEM((PAIR, PAIR), F32),
                        pltpu.VMEM((tblk, PAIR), F32), pltpu.VMEM((tblk, PAIR), F32),
                        pltpu.VMEM((nch, PAIR, PAIR), F32), pltpu.VMEM((nch, PAIR, PAIR), F32)],
        compiler_params=_cparams(("parallel", "parallel", "arbitrary")),
        name="rwkv_scan",
    )(r, k, v, lw, a, b, bonus, g, gng.reshape(1, -1), gnb.reshape(1, -1))


def _relayout_cols(w, axis):
    def sl(a, b):
        idx = [slice(None)] * w.ndim
        idx[axis] = slice(a, b)
        return w[tuple(idx)]
    pad_shape = list(w.shape)
    pad_shape[axis] = LORA_PAD - DECAY_LORA
    z = jnp.zeros(pad_shape, w.dtype)
    lo = 3 * RWKV_W
    rest = lo + DECAY_LORA + ICLR_LORA + GATE_LORA
    return jnp.concatenate(
        [sl(0, lo), sl(rest, w.shape[axis]), sl(lo, lo + DECAY_LORA), z,
         sl(lo + DECAY_LORA, lo + DECAY_LORA + ICLR_LORA), z,
         sl(lo + DECAY_LORA + ICLR_LORA, rest)], axis=axis)


def _pad_rows(w):
    return jnp.concatenate([w, jnp.zeros((LORA_PAD - w.shape[0], w.shape[1]), w.dtype)], axis=0)


def _layer(x, bsz, seq, w_in, w_out, attn_norm, ffn_norm, mu, w0, w_up, a0, a_up, g_up, k_k, k_a, r_k,
           gn_g, gn_b, conv_w, conv_b, ln_g, ln_b, q_norm, k_norm, ffn_up, ffn_conv, ffn_down, tm_mm=1024):
    h = _rmsnorm(x, attn_norm)
    p = _matmul([h], w_in, tm=tm_mm, name="proj_in")
    mu_p, mu_l = mu[:3 * RWKV_W], mu[COL_LORA:]
    r, kt, v, lw, a, b, bonus, g = _rwkv_prep(p, mu_p, mu_l, w0, w_up, a0, a_up, g_up, k_k, k_a,
                                              r_k.reshape(-1), seq)
    y_a = _rwkv_scan(r, kt, v, lw, a, b, bonus, g, gn_g, gn_b, bsz, seq, tblk=min(512, seq))
    y_b = _conformer(p, conv_w, conv_b, ln_g, ln_b, seq)
    y_c = _stick_breaking(p, q_norm, k_norm, bsz, seq)
    x = _matmul([y_a, y_b, y_c], w_out, residual=x, tm=tm_mm, name="proj_out")
    h = _rmsnorm(x, ffn_norm)
    act = _ffn_up(h, ffn_up, ffn_conv, seq, tm=tm_mm)
    return _matmul([act], ffn_down, residual=x, tm=min(512, tm_mm), name="ffn_down")


def kernel(x, w_in, w_out, attn_norm, ffn_norm, rwkv_mu, rwkv_w0, rwkv_w_up, rwkv_a0, rwkv_a_up,
           rwkv_g_up, rwkv_k_k, rwkv_k_a, rwkv_r_k, rwkv_gn_g, rwkv_gn_b, conv_w, conv_b, conv_ln_g,
           conv_ln_b, sb_q_norm, sb_k_norm, ffn_up, ffn_conv, ffn_down):
    bsz, seq, d = x.shape
    x = x.reshape(bsz * seq, d)
    for l in range(w_in.shape[0]):
        mu_full = jnp.concatenate([rwkv_mu[l], jnp.zeros((w_in.shape[2] - rwkv_mu.shape[1],), F32)])
        mu_re = _relayout_cols(mu_full, 0)
        x = _layer(
            x, bsz, seq,
            _relayout_cols(w_in[l], 1).astype(BF16), w_out[l].astype(BF16), attn_norm[l], ffn_norm[l],
            mu_re, rwkv_w0[l], _pad_rows(rwkv_w_up[l]).astype(BF16), rwkv_a0[l],
            _pad_rows(rwkv_a_up[l]).astype(BF16), rwkv_g_up[l].astype(BF16), rwkv_k_k[l], rwkv_k_a[l],
            rwkv_r_k[l], rwkv_gn_g[l], rwkv_gn_b[l], conv_w[l], conv_b[l], conv_ln_g[l], conv_ln_b[l],
            sb_q_norm[l], sb_k_norm[l], ffn_up[l].astype(BF16), ffn_conv[l], ffn_down[l].astype(BF16),
            tm_mm=min(1024, seq))
    return x.reshape(bsz, seq, d)
```

```python
import functools

import jax
import jax.numpy as jnp
from jax import lax
from jax.experimental import pallas as pl
from jax.experimental.pallas import tpu as pltpu

F32 = jnp.float32
BF16 = jnp.bfloat16

D_MODEL = 4096
DEPTH = 4
RWKV_HEAD = 64
RWKV_W = 2048
RWKV_HEADS = RWKV_W // RWKV_HEAD
DECAY_LORA = 96
ICLR_LORA = 96
GATE_LORA = 256
CONV_W = 1024
CONV_K = 31
SB_HEAD = 128
SB_W = 1024
SB_HEADS = SB_W // SB_HEAD
D_FF = 2 * D_MODEL
FFN_CONV_K = 3
NORM_EPS = 1e-6
LN_EPS = 1e-5
GN_EPS = RWKV_HEAD * 1e-5

LANES = 128
LORA_PAD = 128
COL_RKV = 0
COL_CVAL = 3 * RWKV_W
COL_CGATE = COL_CVAL + CONV_W
COL_SBQ = COL_CGATE + CONV_W
COL_SBK = COL_SBQ + SB_W
COL_SBV = COL_SBK + SB_W
COL_LORA = COL_SBV + SB_W
LORA_COLS = 2 * LORA_PAD + GATE_LORA
PROJ_PAD = COL_LORA + LORA_COLS
VMEM_LIMIT = 56 * 1024 * 1024

RWKV_C = 64
PAIR = 2 * RWKV_HEAD


def _cparams(sem):
    return pltpu.CompilerParams(dimension_semantics=sem, vmem_limit_bytes=VMEM_LIMIT)


def _dot(a, b):
    return jnp.dot(a.astype(BF16), b.astype(BF16), preferred_element_type=F32)


def _dot_nt(a, b):
    return lax.dot_general(a.astype(BF16), b.astype(BF16), (((1,), (1,)), ((), ())),
                           preferred_element_type=F32)


def _dot_tn(a, b):
    return lax.dot_general(a.astype(BF16), b.astype(BF16), (((0,), (0,)), ((), ())),
                           preferred_element_type=F32)


def _split3(x):
    hi = x.astype(BF16)
    r1 = x - hi.astype(F32)
    mid = r1.astype(BF16)
    lo = (r1 - mid.astype(F32)).astype(BF16)
    return hi, mid, lo


def _dot_x3(a, b_exact):
    hi, mid, lo = _split3(a)
    b = b_exact.astype(BF16)
    return (jnp.dot(hi, b, preferred_element_type=F32)
            + jnp.dot(mid, b, preferred_element_type=F32)
            + jnp.dot(lo, b, preferred_element_type=F32))


def _x3_dot(a_exact, b):
    hi, mid, lo = _split3(b)
    a = a_exact.astype(BF16)
    return (jnp.dot(a, hi, preferred_element_type=F32)
            + jnp.dot(a, mid, preferred_element_type=F32)
            + jnp.dot(a, lo, preferred_element_type=F32))


def _dot_x2(a, b_exact):
    hi = a.astype(BF16)
    lo = (a - hi.astype(F32)).astype(BF16)
    b = b_exact.astype(BF16)
    return jnp.dot(hi, b, preferred_element_type=F32) + jnp.dot(lo, b, preferred_element_type=F32)


_SB_SUFFIX_DOT = _dot


def _dot_hi(a, b):
    ah = a.astype(BF16)
    al = (a - ah.astype(F32)).astype(BF16)
    bh = b.astype(BF16)
    bl = (b - bh.astype(F32)).astype(BF16)
    return (jnp.dot(ah, bh, preferred_element_type=F32)
            + jnp.dot(al, bh, preferred_element_type=F32)
            + jnp.dot(ah, bl, preferred_element_type=F32))


def _softplus(x):
    return jnp.maximum(x, 0.0) + jnp.log1p(jnp.exp(-jnp.abs(x)))


def _rmsnorm_kernel(x_ref, g_ref, o_ref):
    x = x_ref[...]
    ms = jnp.mean(x * x, axis=-1, keepdims=True)
    o_ref[...] = (x * lax.rsqrt(ms + NORM_EPS) * g_ref[...]).astype(o_ref.dtype)


def _rmsnorm(x, g, tm=256):
    m, d = x.shape
    return pl.pallas_call(
        _rmsnorm_kernel,
        grid=(m // tm,),
        in_specs=[pl.BlockSpec((tm, d), lambda i: (i, 0)),
                  pl.BlockSpec((1, d), lambda i: (0, 0))],
        out_specs=pl.BlockSpec((tm, d), lambda i: (i, 0)),
        out_shape=jax.ShapeDtypeStruct((m, d), BF16),
        compiler_params=_cparams(("parallel",)),
        name="rmsnorm",
    )(x, g.reshape(1, d))


def _mm_kernel(*refs, n_lhs, has_res):
    o_ref = refs[-1]
    acc = None
    for a_ref, w_ref in zip(refs[:n_lhs], refs[n_lhs:2 * n_lhs]):
        d = jnp.dot(a_ref[...], w_ref[...], preferred_element_type=F32)
        acc = d if acc is None else acc + d
    if has_res:
        acc = acc + refs[2 * n_lhs][...]
    o_ref[...] = acc.astype(o_ref.dtype)


def _matmul(lhs_list, w, residual=None, tm=1024, tn=512, name="matmul"):
    m = lhs_list[0].shape[0]
    n = w.shape[1]
    in_specs, args = [], []
    for a in lhs_list:
        in_specs.append(pl.BlockSpec((tm, a.shape[1]), lambda i, j: (i, 0)))
        args.append(a)
    row = 0
    for a in lhs_list:
        k = a.shape[1]
        assert row % k == 0
        in_specs.append(pl.BlockSpec((k, tn), functools.partial(lambda i, j, rb: (rb, j), rb=row // k)))
        args.append(w)
        row += k
    assert row == w.shape[0]
    if residual is not None:
        in_specs.append(pl.BlockSpec((tm, tn), lambda i, j: (i, j)))
        args.append(residual)
    return pl.pallas_call(
        functools.partial(_mm_kernel, n_lhs=len(lhs_list), has_res=residual is not None),
        grid=(m // tm, n // tn),
        in_specs=in_specs,
        out_specs=pl.BlockSpec((tm, tn), lambda i, j: (i, j)),
        out_shape=jax.ShapeDtypeStruct((m, n), F32),
        compiler_params=_cparams(("parallel", "arbitrary")),
        name=name,
    )(*args)


FFN_HALO = 16


def _ffn_up_kernel(h_ref, halo_ref, wg_ref, wv_ref, cg_ref, cv_ref, o_ref, sg, sv, *, tm, seq):
    i = pl.program_id(0)
    first = (i * tm) % seq == 0
    h = h_ref[...]
    halo = jnp.where(first, jnp.zeros_like(halo_ref[...]), halo_ref[...])

    def conv(w_ref, c_ref, s):
        s[0:FFN_HALO, :] = jnp.dot(halo, w_ref[...], preferred_element_type=F32)
        s[FFN_HALO:FFN_HALO + tm, :] = jnp.dot(h, w_ref[...], preferred_element_type=F32)
        acc = None
        for k in range(FFN_CONV_K):
            off = FFN_HALO - (FFN_CONV_K - 1) + k
            term = c_ref[k:k + 1, :] * s[off:off + tm, :]
            acc = term if acc is None else acc + term
        return acc

    g = conv(wg_ref, cg_ref, sg)
    v = conv(wv_ref, cv_ref, sv)
    o_ref[...] = (g * jax.nn.sigmoid(g) * v).astype(o_ref.dtype)


def _ffn_up(h, w_up, conv, seq, tm=1024, tn=512):
    m, d = h.shape
    nj = D_FF // tn
    return pl.pallas_call(
        functools.partial(_ffn_up_kernel, tm=tm, seq=seq),
        grid=(m // tm, nj),
        in_specs=[
            pl.BlockSpec((tm, d), lambda i, j: (i, 0)),
            pl.BlockSpec((FFN_HALO, d), lambda i, j: (jnp.maximum(i * (tm // FFN_HALO) - 1, 0), 0)),
            pl.BlockSpec((d, tn), lambda i, j: (0, j)),
            pl.BlockSpec((d, tn), lambda i, j: (0, j + nj)),
            pl.BlockSpec((FFN_CONV_K, tn), lambda i, j: (0, j)),
            pl.BlockSpec((FFN_CONV_K, tn), lambda i, j: (0, j + nj)),
        ],
        out_specs=pl.BlockSpec((tm, tn), lambda i, j: (i, j)),
        out_shape=jax.ShapeDtypeStruct((m, D_FF), BF16),
        scratch_shapes=[pltpu.VMEM((FFN_HALO + tm, tn), F32),
                        pltpu.VMEM((FFN_HALO + tm, tn), F32)],
        compiler_params=_cparams(("parallel", "arbitrary")),
        name="ffn_up_conv_gate",
    )(h, h, w_up, w_up, conv, conv)


CONF_HALO = 32
CONF_ROWS = 32
CONF_COLS = 256


def _conformer_kernel(val_ref, gate_ref, hval_ref, hgate_ref, cw_ref, cb_ref, lng_ref, lnb_ref,
                      o_ref, glu_s, u_s, *, tm, seq):
    i = pl.program_id(0)
    first = (i * tm) % seq == 0
    hglu = hval_ref[...] * jax.nn.sigmoid(hgate_ref[...])
    glu_s[0:CONF_HALO, :] = jnp.where(first, jnp.zeros_like(hglu), hglu)
    glu_s[CONF_HALO:CONF_HALO + tm, :] = val_ref[...] * jax.nn.sigmoid(gate_ref[...])
    base = CONF_HALO - (CONV_K - 1)
    for r0 in range(0, tm, CONF_ROWS):
        for c0 in range(0, CONV_W, CONF_COLS):
            acc = jnp.zeros((CONF_ROWS, CONF_COLS), F32) + cb_ref[:, c0:c0 + CONF_COLS]
            for k in range(CONV_K):
                acc = acc + cw_ref[k:k + 1, c0:c0 + CONF_COLS] * glu_s[r0 + base + k:r0 + base + k + CONF_ROWS,
                                                                   c0:c0 + CONF_COLS]
            u_s[r0:r0 + CONF_ROWS, c0:c0 + CONF_COLS] = acc
    u = u_s[...]
    mean = jnp.mean(u, axis=-1, keepdims=True)
    d = u - mean
    var = jnp.mean(d * d, axis=-1, keepdims=True)
    y = d * lax.rsqrt(var + LN_EPS) * lng_ref[...] + lnb_ref[...]
    o_ref[...] = (y * jax.nn.sigmoid(y)).astype(o_ref.dtype)


def _conformer(p, cw, cb, lng, lnb, seq, tm=256):
    m = p.shape[0]
    vb, gb = COL_CVAL // CONV_W, COL_CGATE // CONV_W
    hmap = lambda i: jnp.maximum(i * (tm // CONF_HALO) - 1, 0)
    return pl.pallas_call(
        functools.partial(_conformer_kernel, tm=tm, seq=seq),
        grid=(m // tm,),
        in_specs=[
            pl.BlockSpec((tm, CONV_W), lambda i: (i, vb)),
            pl.BlockSpec((tm, CONV_W), lambda i: (i, gb)),
            pl.BlockSpec((CONF_HALO, CONV_W), lambda i: (hmap(i), vb)),
            pl.BlockSpec((CONF_HALO, CONV_W), lambda i: (hmap(i), gb)),
            pl.BlockSpec((CONV_K, CONV_W), lambda i: (0, 0)),
            pl.BlockSpec((1, CONV_W), lambda i: (0, 0)),
            pl.BlockSpec((1, CONV_W), lambda i: (0, 0)),
            pl.BlockSpec((1, CONV_W), lambda i: (0, 0)),
        ],
        out_specs=pl.BlockSpec((tm, CONV_W), lambda i: (i, 0)),
        out_shape=jax.ShapeDtypeStruct((m, CONV_W), BF16),
        scratch_shapes=[pltpu.VMEM((CONF_HALO + tm, CONV_W), F32),
                        pltpu.VMEM((tm, CONV_W), F32)],
        compiler_params=_cparams(("parallel",)),
        name="conformer_conv",
    )(p, p, p, p, cw, cb.reshape(1, -1), lng.reshape(1, -1), lnb.reshape(1, -1))


SB_BQ = 256
SB_BK = 256
SB_PREP_ROWS = 512
LOG2E = 1.4426950408889634
LN2 = 0.6931471805599453


def _sb_kernel(q_ref, k_ref, v_ref, qg_ref, kg_ref, o_ref, kn_s, vb_s, *, seq):
    qi = pl.program_id(2)

    @pl.when(qi == 0)
    def _():
        def body(c, carry):
            sl = pl.ds(pl.multiple_of(c * SB_PREP_ROWS, SB_PREP_ROWS), SB_PREP_ROWS)
            k = k_ref[sl, :]
            ms = jnp.mean(k * k, axis=-1, keepdims=True)
            kn_s[sl, :] = (k * lax.rsqrt(ms + NORM_EPS) * kg_ref[...]).astype(BF16)
            vb_s[sl, :] = v_ref[sl, :].astype(BF16)
            return carry
        lax.fori_loop(0, seq // SB_PREP_ROWS, body, 0)

    q = q_ref[...]
    ms = jnp.mean(q * q, axis=-1, keepdims=True)
    q = (q * lax.rsqrt(ms + NORM_EPS) * qg_ref[...] * (SB_HEAD ** -0.5 * LOG2E)).astype(BF16)

    row = lax.broadcasted_iota(jnp.int32, (SB_BK, SB_BK), 0)
    col = lax.broadcasted_iota(jnp.int32, (SB_BK, SB_BK), 1)
    later = (row > col).astype(BF16)
    causal = col < row

    def block(k0, nsub, c, acc, masked):
        sl = pl.ds(pl.multiple_of(k0, SB_BK), nsub * SB_BK)
        z = lax.dot_general(q, kn_s[sl, :], (((1,), (1,)), ((), ())), preferred_element_type=F32)
        sp = jnp.maximum(z, 0.0) + jnp.log(1.0 + jnp.exp2(-jnp.abs(z))) * (1.0 / LN2)
        if masked:
            sp = jnp.where(causal, sp, 0.0)
        atts = [None] * nsub
        for j in reversed(range(nsub)):
            cols = slice(j * SB_BK, (j + 1) * SB_BK)
            sp_j = sp[:, cols]
            suffix = _SB_SUFFIX_DOT(sp_j, later)
            att = jnp.exp2((z[:, cols] - sp_j) - (suffix + c))
            if masked:
                att = jnp.where(causal, att, 0.0)
            atts[j] = att.astype(BF16)
            c = c + jnp.sum(sp_j, axis=-1, keepdims=True)
        att = atts[0] if nsub == 1 else jnp.concatenate(atts, axis=1)
        acc = acc + jnp.dot(att, vb_s[sl, :], preferred_element_type=F32)
        return c, acc

    c0 = jnp.zeros((SB_BQ, 1), F32)
    acc0 = jnp.zeros((SB_BQ, SB_HEAD), F32)
    carry = block(qi * SB_BK, 1, c0, acc0, True)
    odd = qi % 2
    carry = lax.fori_loop(0, odd, lambda n, cr: block((qi - 1) * SB_BK, 1, cr[0], cr[1], False), carry)

    def pair(n, cr):
        return block((qi - odd - 2 - 2 * n) * SB_BK, 2, cr[0], cr[1], False)

    c, acc = lax.fori_loop(0, qi // 2, pair, carry)
    o_ref[...] = acc.astype(o_ref.dtype)


def _stick_breaking(p, qg, kg, bsz, seq):
    assert SB_BQ == SB_BK
    nq = seq // SB_BQ
    qb, kb, vb = COL_SBQ // SB_HEAD, COL_SBK // SB_HEAD, COL_SBV // SB_HEAD
    return pl.pallas_call(
        functools.partial(_sb_kernel, seq=seq),
        grid=(bsz, SB_HEADS, nq),
        in_specs=[
            pl.BlockSpec((SB_BQ, SB_HEAD), lambda b, h, q: (b * nq + q, qb + h)),
            pl.BlockSpec((seq, SB_HEAD), lambda b, h, q: (b, kb + h)),
            pl.BlockSpec((seq, SB_HEAD), lambda b, h, q: (b, vb + h)),
            pl.BlockSpec((1, SB_HEAD), lambda b, h, q: (0, 0)),
            pl.BlockSpec((1, SB_HEAD), lambda b, h, q: (0, 0)),
        ],
        out_specs=pl.BlockSpec((SB_BQ, SB_HEAD), lambda b, h, q: (b * nq + q, h)),
        out_shape=jax.ShapeDtypeStruct((bsz * seq, SB_W), BF16),
        scratch_shapes=[pltpu.VMEM((seq, SB_HEAD), BF16), pltpu.VMEM((seq, SB_HEAD), BF16)],
        compiler_params=_cparams(("parallel", "parallel", "arbitrary")),
        name="stick_breaking",
    )(p, p, p, qg.reshape(1, -1), kg.reshape(1, -1))


PREP_HALO = 8


def _head_ones():
    r = lax.broadcasted_iota(jnp.int32, (PAIR, PAIR), 0) // RWKV_HEAD
    c = lax.broadcasted_iota(jnp.int32, (PAIR, PAIR), 1) // RWKV_HEAD
    return (r == c).astype(BF16)


def _rwkv_prep_kernel(p_ref, lora_ref, hp_ref, hlora_ref, mu_ref, mul_ref, w0_ref, wup_ref, a0_ref,
                      aup_ref, gup_ref, kk_ref, ka_ref, rk_ref,
                      r_o, k_o, v_o, lw_o, a_o, b_o, bonus_o, g_o, *, tm, seq):
    i = pl.program_id(0)
    first = (i * tm) % seq == 0

    def shifted(x_ref, h_ref, m_ref, c0, c1):
        x = x_ref[:, c0:c1]
        prev = pltpu.roll(x, 1, 0)
        row = lax.broadcasted_iota(jnp.int32, x.shape, 0)
        last = h_ref[PREP_HALO - 1:PREP_HALO, c0:c1]
        last = jnp.where(first, jnp.zeros_like(last), last)
        prev = jnp.where(row == 0, last, prev)
        return x + m_ref[:, c0:c1] * (prev - x)

    lo = shifted(lora_ref, hlora_ref, mul_ref, 0, LORA_COLS)
    w_lo = lo[:, 0:LORA_PAD]
    a_lo = lo[:, LORA_PAD:2 * LORA_PAD]
    g_lo = lo[:, 2 * LORA_PAD:]
    ones = _head_ones()

    for c0 in range(0, RWKV_W, 512):
        c1 = c0 + 512
        r = shifted(p_ref, hp_ref, mu_ref, c0, c1)
        k = shifted(p_ref, hp_ref, mu_ref, RWKV_W + c0, RWKV_W + c1)
        v = shifted(p_ref, hp_ref, mu_ref, 2 * RWKV_W + c0, 2 * RWKV_W + c1)
        y = w0_ref[:, c0:c1] + _dot(jnp.tanh(w_lo), wup_ref[:, c0:c1])
        log_w = -_softplus(-y) - 0.5
        lw_o[:, c0:c1] = -jnp.exp(log_w)
        iclr = jax.nn.sigmoid(a0_ref[:, c0:c1] + _dot(a_lo, aup_ref[:, c0:c1]))
        g_o[:, c0:c1] = _dot(jax.nn.sigmoid(g_lo), gup_ref[:, c0:c1])
        kk = k * kk_ref[:, c0:c1]
        kt = k * (1.0 + (iclr - 1.0) * ka_ref[:, c0:c1])
        rkr = r * kt * rk_ref[:, c0:c1]
        for t0 in range(0, 512, PAIR):
            t1 = t0 + PAIR
            kk_t = kk[:, t0:t1]
            ss = _dot_x3(kk_t * kk_t, ones)
            kk_n = kk_t * lax.rsqrt(jnp.maximum(ss, 1e-24))
            a_o[:, c0 + t0:c0 + t1] = -kk_n
            b_o[:, c0 + t0:c0 + t1] = kk_n * iclr[:, t0:t1]
            bonus_o[:, c0 + t0:c0 + t1] = _dot_x3(rkr[:, t0:t1], ones) * v[:, t0:t1]
        r_o[:, c0:c1] = r
        k_o[:, c0:c1] = kt
        v_o[:, c0:c1] = v


def _rwkv_prep(p, mu, mul, w0, wup, a0, aup, gup, kk, ka, rk, seq, tm=128):
    m = p.shape[0]
    lb = COL_LORA // LORA_COLS
    hmap = lambda i: jnp.maximum(i * (tm // PREP_HALO) - 1, 0)
    row = lambda a: a.reshape(1, -1)
    vec = pl.BlockSpec((1, RWKV_W), lambda i: (0, 0))
    out = jax.ShapeDtypeStruct((m, RWKV_W), F32)
    ospec = pl.BlockSpec((tm, RWKV_W), lambda i: (i, 0))
    return pl.pallas_call(
        functools.partial(_rwkv_prep_kernel, tm=tm, seq=seq),
        grid=(m // tm,),
        in_specs=[
            pl.BlockSpec((tm, 3 * RWKV_W), lambda i: (i, 0)),
            pl.BlockSpec((tm, LORA_COLS), lambda i: (i, lb)),
            pl.BlockSpec((PREP_HALO, 3 * RWKV_W), lambda i: (hmap(i), 0)),
            pl.BlockSpec((PREP_HALO, LORA_COLS), lambda i: (hmap(i), lb)),
            pl.BlockSpec((1, 3 * RWKV_W), lambda i: (0, 0)),
            pl.BlockSpec((1, LORA_COLS), lambda i: (0, 0)),
            vec,
            pl.BlockSpec((LORA_PAD, RWKV_W), lambda i: (0, 0)),
            vec,
            pl.BlockSpec((LORA_PAD, RWKV_W), lambda i: (0, 0)),
            pl.BlockSpec((GATE_LORA, RWKV_W), lambda i: (0, 0)),
            vec, vec, vec,
        ],
        out_specs=[ospec] * 8,
        out_shape=[out] * 8,
        compiler_params=_cparams(("parallel",)),
        name="rwkv_prep",
    )(p, p, p, p, row(mu), row(mul), row(w0), wup, row(a0), aup, gup, row(kk), row(ka), row(rk))


SCAN_PAIRS = 2
GN_ROWS = 256


def _rwkv_scan_kernel(r_ref, k_ref, v_ref, lw_ref, a_ref, b_ref, bonus_ref, g_ref, gng_ref, gnb_ref,
                      o_ref, s_ref, rq_s, oin_s, tr_s, dl_s, *, tblk):
    C = RWKV_C
    t_id = pl.program_id(2)

    @pl.when(t_id == 0)
    def _():
        s_ref[...] = jnp.zeros_like(s_ref)

    lane = lax.broadcasted_iota(jnp.int32, (1, PAIR), 1)
    m0 = (lane < RWKV_HEAD).astype(F32)
    m1 = 1.0 - m0
    ti = lax.broadcasted_iota(jnp.int32, (C, C), 0)
    ii = lax.broadcasted_iota(jnp.int32, (C, C), 1)
    tri_incl = (ii <= ti).astype(BF16)
    t2 = lax.broadcasted_iota(jnp.int32, (C, 2 * C), 0)
    i2 = lax.broadcasted_iota(jnp.int32, (C, 2 * C), 1) % C
    strict2 = (i2 < t2).astype(F32)
    incl2 = (i2 <= t2).astype(F32)
    rr = lax.broadcasted_iota(jnp.int32, (PAIR, PAIR), 0)
    cc = lax.broadcasted_iota(jnp.int32, (PAIR, PAIR), 1)
    bdmask = (rr // RWKV_HEAD == cc // RWKV_HEAD).astype(F32)
    eye = (rr == cc).astype(F32)
    ones = _head_ones()

    def stack_heads(x):
        return jnp.concatenate([x * m0, x * m1], axis=0)

    def chunk_pair(c, pi):
        sl = pl.ds(pl.multiple_of(c * C, C), C)
        ln = slice(pi * PAIR, (pi + 1) * PAIR)
        lw = lw_ref[sl, ln]
        r, k, v, a, b = r_ref[sl, ln], k_ref[sl, ln], v_ref[sl, ln], a_ref[sl, ln], b_ref[sl, ln]
        cum = _x3_dot(tri_incl, lw)
        tot = cum[C - 1:C, :]
        e_neg = jnp.exp(-cum)
        e_end = jnp.exp(tot - cum)
        at = a * jnp.exp(cum - lw)
        rt = r * jnp.exp(cum)
        bh, kh = b * e_neg, k * e_neg
        bend, kend = b * e_end, k * e_end
        gram = _dot_nt(jnp.concatenate([at, rt], axis=0),
                       jnp.concatenate([stack_heads(bh), stack_heads(kh)], axis=0))
        m_ab = gram[0:C, 0:2 * C] * strict2
        m_ak = gram[0:C, 2 * C:] * strict2
        q_b = gram[C:, 0:2 * C] * incl2
        q_k = gram[C:, 2 * C:] * incl2
        vv = stack_heads(v)
        u_rhs = _dot(m_ak, vv)
        bd = stack_heads(m_ab)
        t_inv = eye + bd
        pw = bd
        for _ in range(5):
            pw = _dot(pw, pw)
            t_inv = t_inv + _dot(t_inv, pw)
        t_ls = t_inv[0:C, :] + t_inv[C:, :]
        sol = _dot(t_ls, jnp.concatenate(
            [jnp.concatenate([at * m0, u_rhs * m0], axis=1),
             jnp.concatenate([at * m1, u_rhs * m1], axis=1)], axis=0))
        wa, u0 = sol[:, 0:PAIR], sol[:, PAIR:]
        y = _dot(q_b, jnp.concatenate(
            [jnp.concatenate([wa * m0, u0 * m0], axis=1),
             jnp.concatenate([wa * m1, u0 * m1], axis=1)], axis=0))
        rq_s[sl, ln] = rt + y[:, 0:PAIR]
        oin_s[sl, ln] = y[:, PAIR:] + _dot(q_k, vv)
        tr_s[pi, c] = _dot_tn(wa, bend) * bdmask + eye * jnp.exp(tot)
        dl_s[pi, c] = (_dot_tn(u0, bend) + _dot_tn(v, kend)) * bdmask

    def chunk(c, carry):
        for pi in range(SCAN_PAIRS):
            chunk_pair(c, pi)
        return carry

    lax.fori_loop(0, tblk // C, chunk, 0)

    def scan(c, carry):
        sl = pl.ds(pl.multiple_of(c * C, C), C)
        for pi in range(SCAN_PAIRS):
            ln = slice(pi * PAIR, (pi + 1) * PAIR)
            s = s_ref[pi]
            oin_s[sl, ln] = _dot_nt(rq_s[sl, ln], s) + oin_s[sl, ln]
            s_ref[pi] = _dot(s, tr_s[pi, c]) + dl_s[pi, c]
        return carry

    lax.fori_loop(0, tblk // C, scan, 0)

    for pi in range(SCAN_PAIRS):
        ln = slice(pi * PAIR, (pi + 1) * PAIR)
        for r0 in range(0, tblk, GN_ROWS):
            rows = slice(r0, r0 + GN_ROWS)
            o = oin_s[rows, ln]
            mean = _dot_x3(o, ones) * (1.0 / RWKV_HEAD)
            d = o - mean
            var = _dot_x3(d * d, ones) * (1.0 / RWKV_HEAD)
            on = d * lax.rsqrt(var + GN_EPS) * gng_ref[:, ln] + gnb_ref[:, ln]
            o_ref[rows, ln] = ((on + bonus_ref[rows, ln]) * g_ref[rows, ln]).astype(o_ref.dtype)


def _rwkv_scan(r, k, v, lw, a, b, bonus, g, gng, gnb, bsz, seq, tblk=512):
    nt = seq // tblk
    width = SCAN_PAIRS * PAIR
    tile = pl.BlockSpec((tblk, width), lambda bi, h, t: (bi * nt + t, h))
    vec = pl.BlockSpec((1, width), lambda bi, h, t: (0, h))
    nch = tblk // RWKV_C
    return pl.pallas_call(
        functools.partial(_rwkv_scan_kernel, tblk=tblk),
        grid=(bsz, RWKV_W // width, nt),
        in_specs=[tile] * 8 + [vec, vec],
        out_specs=tile,
        out_shape=jax.ShapeDtypeStruct((bsz * seq, RWKV_W), BF16),
        scratch_shapes=[pltpu.VMEM((SCAN_PAIRS, PAIR, PAIR), F32),
                        pltpu.VMEM((tblk, width), F32), pltpu.VMEM((tblk, width), F32),
                        pltpu.VMEM((SCAN_PAIRS, nch, PAIR, PAIR), F32),
                        pltpu.VMEM((SCAN_PAIRS, nch, PAIR, PAIR), F32)],
        compiler_params=_cparams(("parallel", "parallel", "arbitrary")),
        name="rwkv_scan",
    )(r, k, v, lw, a, b, bonus, g, gng.reshape(1, -1), gnb.reshape(1, -1))


QUAD_HEADS = 2
QUAD = QUAD_HEADS * RWKV_HEAD
SCAN_GROUPS = 8


def _rwkv_quad_kernel(r_ref, k_ref, v_ref, lw_ref, a_ref, b_ref, bonus_ref, g_ref, gng_ref, gnb_ref,
                      o_ref, s_ref, rq_s, oin_s, tr_s, dl_s, *, tblk):
    C = RWKV_C
    HC = QUAD_HEADS * C
    G = range(SCAN_GROUPS)
    t_id = pl.program_id(2)

    @pl.when(t_id == 0)
    def _():
        s_ref[...] = jnp.zeros_like(s_ref)

    lane = lax.broadcasted_iota(jnp.int32, (1, QUAD), 1) // RWKV_HEAD
    hm = [(lane == h).astype(F32) for h in range(QUAD_HEADS)]
    ti = lax.broadcasted_iota(jnp.int32, (C, C), 0)
    ii = lax.broadcasted_iota(jnp.int32, (C, C), 1)
    tri_incl = (ii <= ti).astype(BF16)
    t4 = lax.broadcasted_iota(jnp.int32, (C, HC), 0)
    i4 = lax.broadcasted_iota(jnp.int32, (C, HC), 1) % C
    strict4 = (i4 < t4).astype(F32)
    incl4 = (i4 <= t4).astype(F32)
    rr = lax.broadcasted_iota(jnp.int32, (QUAD, QUAD), 0)
    cc = lax.broadcasted_iota(jnp.int32, (QUAD, QUAD), 1)
    bdmask = (rr // RWKV_HEAD == cc // RWKV_HEAD).astype(F32)
    eye = (rr == cc).astype(F32)
    ones = (rr // RWKV_HEAD == cc // RWKV_HEAD).astype(BF16)

    def stack_heads(x):
        return jnp.concatenate([x * m for m in hm], axis=0)

    def stack_heads2(x, y):
        return jnp.concatenate([jnp.concatenate([x * m, y * m], axis=1) for m in hm], axis=0)

    def chunk(c, carry):
        sl = pl.ds(pl.multiple_of(c * C, C), C)
        lns = [slice(g * QUAD, (g + 1) * QUAD) for g in G]
        lw = [lw_ref[sl, ln] for ln in lns]
        cum = [_x3_dot(tri_incl, x) for x in lw]
        tot = [x[C - 1:C, :] for x in cum]
        e_neg = [jnp.exp(-x) for x in cum]
        e_end = [jnp.exp(t - x) for t, x in zip(tot, cum)]
        at = [a_ref[sl, ln] * jnp.exp(x - w) for ln, x, w in zip(lns, cum, lw)]
        rt = [r_ref[sl, ln] * jnp.exp(x) for ln, x in zip(lns, cum)]
        b = [b_ref[sl, ln] for ln in lns]
        k = [k_ref[sl, ln] for ln in lns]
        v = [v_ref[sl, ln] for ln in lns]
        gram = [_dot_nt(jnp.concatenate([at[g], rt[g]], axis=0),
                        jnp.concatenate([stack_heads(b[g] * e_neg[g]), stack_heads(k[g] * e_neg[g])], axis=0))
                for g in G]
        m_ak = [x[0:C, HC:] * strict4 for x in gram]
        q_b = [x[C:, 0:HC] * incl4 for x in gram]
        q_k = [x[C:, HC:] * incl4 for x in gram]
        vv = [stack_heads(x) for x in v]
        u_rhs = [_dot(m_ak[g], vv[g]) for g in G]
        pw = [stack_heads(x[0:C, 0:HC] * strict4) for x in gram]
        t_inv = [eye + x for x in pw]
        for _ in range(5):
            pw = [_dot(x, x) for x in pw]
            t_inv = [t + _dot(t, x) for t, x in zip(t_inv, pw)]
        t_ls = [sum(t[h * C:(h + 1) * C, :] for h in range(1, QUAD_HEADS)) + t[0:C, :] for t in t_inv]
        sol = [_dot(t_ls[g], stack_heads2(at[g], u_rhs[g])) for g in G]
        wa = [x[:, 0:QUAD] for x in sol]
        u0 = [x[:, QUAD:] for x in sol]
        y = [_dot(q_b[g], stack_heads2(wa[g], u0[g])) for g in G]
        o_in2 = [_dot(q_k[g], vv[g]) for g in G]
        bend = [b[g] * e_end[g] for g in G]
        tr = [_dot_tn(wa[g], bend[g]) for g in G]
        dl = [_dot_tn(jnp.concatenate([u0[g], v[g]], axis=0),
                      jnp.concatenate([bend[g], k[g] * e_end[g]], axis=0)) for g in G]
        for g in G:
            rq_s[sl, lns[g]] = rt[g] + y[g][:, 0:QUAD]
            oin_s[sl, lns[g]] = y[g][:, QUAD:] + o_in2[g]
            tr_s[g, c] = tr[g] * bdmask + eye * jnp.exp(tot[g])
            dl_s[g, c] = dl[g] * bdmask
        return carry

    lax.fori_loop(0, tblk // C, chunk, 0)

    def scan(c, carry):
        sl = pl.ds(pl.multiple_of(c * C, C), C)
        for g in G:
            ln = slice(g * QUAD, (g + 1) * QUAD)
            s = s_ref[g]
            oin_s[sl, ln] = _dot_nt(rq_s[sl, ln], s) + oin_s[sl, ln]
            s_ref[g] = _dot(s, tr_s[g, c]) + dl_s[g, c]
        return carry

    lax.fori_loop(0, tblk // C, scan, 0)

    for g in G:
        ln = slice(g * QUAD, (g + 1) * QUAD)
        for r0 in range(0, tblk, GN_ROWS):
            rows = slice(r0, r0 + GN_ROWS)
            o = oin_s[rows, ln]
            mean = _dot_x3(o, ones) * (1.0 / RWKV_HEAD)
            d = o - mean
            var = _dot_x3(d * d, ones) * (1.0 / RWKV_HEAD)
            on = d * lax.rsqrt(var + GN_EPS) * gng_ref[:, ln] + gnb_ref[:, ln]
            o_ref[rows, ln] = ((on + bonus_ref[rows, ln]) * g_ref[rows, ln]).astype(o_ref.dtype)


def _rwkv_quad_scan(r, k, v, lw, a, b, bonus, g, gng, gnb, bsz, seq, tblk=512):
    nt = seq // tblk
    width = SCAN_GROUPS * QUAD
    tile = pl.BlockSpec((tblk, width), lambda bi, h, t: (bi * nt + t, h))
    vec = pl.BlockSpec((1, width), lambda bi, h, t: (0, h))
    nch = tblk // RWKV_C
    return pl.pallas_call(
        functools.partial(_rwkv_quad_kernel, tblk=tblk),
        grid=(bsz, RWKV_W // width, nt),
        in_specs=[tile] * 8 + [vec, vec],
        out_specs=tile,
        out_shape=jax.ShapeDtypeStruct((bsz * seq, RWKV_W), BF16),
        scratch_shapes=[pltpu.VMEM((SCAN_GROUPS, QUAD, QUAD), F32),
                        pltpu.VMEM((tblk, width), F32), pltpu.VMEM((tblk, width), F32),
                        pltpu.VMEM((SCAN_GROUPS, nch, QUAD, QUAD), F32),
                        pltpu.VMEM((SCAN_GROUPS, nch, QUAD, QUAD), F32)],
        compiler_params=_cparams(("parallel", "parallel", "arbitrary")),
        name="rwkv_scan",
    )(r, k, v, lw, a, b, bonus, g, gng.reshape(1, -1), gnb.reshape(1, -1))


def _relayout_cols(w, axis):
    def sl(a, b):
        idx = [slice(None)] * w.ndim
        idx[axis] = slice(a, b)
        return w[tuple(idx)]
    pad_shape = list(w.shape)
    pad_shape[axis] = LORA_PAD - DECAY_LORA
    z = jnp.zeros(pad_shape, w.dtype)
    lo = 3 * RWKV_W
    rest = lo + DECAY_LORA + ICLR_LORA + GATE_LORA
    return jnp.concatenate(
        [sl(0, lo), sl(rest, w.shape[axis]), sl(lo, lo + DECAY_LORA), z,
         sl(lo + DECAY_LORA, lo + DECAY_LORA + ICLR_LORA), z,
         sl(lo + DECAY_LORA + ICLR_LORA, rest)], axis=axis)


def _pad_rows(w):
    return jnp.concatenate([w, jnp.zeros((LORA_PAD - w.shape[0], w.shape[1]), w.dtype)], axis=0)


def _layer(x, bsz, seq, w_in, w_out, attn_norm, ffn_norm, mu, w0, w_up, a0, a_up, g_up, k_k, k_a, r_k,
           gn_g, gn_b, conv_w, conv_b, ln_g, ln_b, q_norm, k_norm, ffn_up, ffn_conv, ffn_down, tm_mm=1024):
    h = _rmsnorm(x, attn_norm)
    p = _matmul([h], w_in, tm=tm_mm, name="proj_in")
    mu_p, mu_l = mu[:3 * RWKV_W], mu[COL_LORA:]
    r, kt, v, lw, a, b, bonus, g = _rwkv_prep(p, mu_p, mu_l, w0, w_up, a0, a_up, g_up, k_k, k_a,
                                              r_k.reshape(-1), seq)
    y_a = _rwkv_quad_scan(r, kt, v, lw, a, b, bonus, g, gn_g, gn_b, bsz, seq, tblk=min(512, seq))
    y_b = _conformer(p, conv_w, conv_b, ln_g, ln_b, seq)
    y_c = _stick_breaking(p, q_norm, k_norm, bsz, seq)
    x = _matmul([y_a, y_b, y_c], w_out, residual=x, tm=tm_mm, name="proj_out")
    h = _rmsnorm(x, ffn_norm)
    act = _ffn_up(h, ffn_up, ffn_conv, seq, tm=tm_mm)
    return _matmul([act], ffn_down, residual=x, tm=min(512, tm_mm), name="ffn_down")


def kernel(x, w_in, w_out, attn_norm, ffn_norm, rwkv_mu, rwkv_w0, rwkv_w_up, rwkv_a0, rwkv_a_up,
           rwkv_g_up, rwkv_k_k, rwkv_k_a, rwkv_r_k, rwkv_gn_g, rwkv_gn_b, conv_w, conv_b, conv_ln_g,
           conv_ln_b, sb_q_norm, sb_k_norm, ffn_up, ffn_conv, ffn_down):
    bsz, seq, d = x.shape
    x = x.reshape(bsz * seq, d)
    for l in range(w_in.shape[0]):
        mu_full = jnp.concatenate([rwkv_mu[l], jnp.zeros((w_in.shape[2] - rwkv_mu.shape[1],), F32)])
        mu_re = _relayout_cols(mu_full, 0)
        x = _layer(
            x, bsz, seq,
            _relayout_cols(w_in[l], 1).astype(BF16), w_out[l].astype(BF16), attn_norm[l], ffn_norm[l],
            mu_re, rwkv_w0[l], _pad_rows(rwkv_w_up[l]).astype(BF16), rwkv_a0[l],
            _pad_rows(rwkv_a_up[l]).astype(BF16), rwkv_g_up[l].astype(BF16), rwkv_k_k[l], rwkv_k_a[l],
            rwkv_r_k[l], rwkv_gn_g[l], rwkv_gn_b[l], conv_w[l], conv_b[l], conv_ln_g[l], conv_ln_b[l],
            sb_q_norm[l], sb_k_norm[l], ffn_up[l].astype(BF16), ffn_conv[l], ffn_down[l].astype(BF16),
            tm_mm=min(1024, seq))
    return x.reshape(bsz, seq, d)
```

```python
import functools

import jax
import jax.numpy as jnp
from jax import lax
from jax.experimental import pallas as pl
from jax.experimental.pallas import tpu as pltpu

F32 = jnp.float32
BF16 = jnp.bfloat16

D_MODEL = 4096
DEPTH = 4
RWKV_HEAD = 64
RWKV_W = 2048
RWKV_HEADS = RWKV_W // RWKV_HEAD
DECAY_LORA = 96
ICLR_LORA = 96
GATE_LORA = 256
CONV_W = 1024
CONV_K = 31
SB_HEAD = 128
SB_W = 1024
SB_HEADS = SB_W // SB_HEAD
D_FF = 2 * D_MODEL
FFN_CONV_K = 3
NORM_EPS = 1e-6
LN_EPS = 1e-5
GN_EPS = RWKV_HEAD * 1e-5

LANES = 128
LORA_PAD = 128
COL_RKV = 0
COL_CVAL = 3 * RWKV_W
COL_CGATE = COL_CVAL + CONV_W
COL_SBQ = COL_CGATE + CONV_W
COL_SBK = COL_SBQ + SB_W
COL_SBV = COL_SBK + SB_W
COL_LORA = COL_SBV + SB_W
LORA_COLS = 2 * LORA_PAD + GATE_LORA
PROJ_PAD = COL_LORA + LORA_COLS
VMEM_LIMIT = 56 * 1024 * 1024

RWKV_C = 64
PAIR = 2 * RWKV_HEAD


def _cparams(sem):
    return pltpu.CompilerParams(dimension_semantics=sem, vmem_limit_bytes=VMEM_LIMIT)


def _dot(a, b):
    return jnp.dot(a.astype(BF16), b.astype(BF16), preferred_element_type=F32)


def _dot_nt(a, b):
    return lax.dot_general(a.astype(BF16), b.astype(BF16), (((1,), (1,)), ((), ())),
                           preferred_element_type=F32)


def _dot_tn(a, b):
    return lax.dot_general(a.astype(BF16), b.astype(BF16), (((0,), (0,)), ((), ())),
                           preferred_element_type=F32)


def _split3(x):
    hi = x.astype(BF16)
    r1 = x - hi.astype(F32)
    mid = r1.astype(BF16)
    lo = (r1 - mid.astype(F32)).astype(BF16)
    return hi, mid, lo


def _dot_x3(a, b_exact):
    hi, mid, lo = _split3(a)
    b = b_exact.astype(BF16)
    return (jnp.dot(hi, b, preferred_element_type=F32)
            + jnp.dot(mid, b, preferred_element_type=F32)
            + jnp.dot(lo, b, preferred_element_type=F32))


def _x3_dot(a_exact, b):
    hi, mid, lo = _split3(b)
    a = a_exact.astype(BF16)
    return (jnp.dot(a, hi, preferred_element_type=F32)
            + jnp.dot(a, mid, preferred_element_type=F32)
            + jnp.dot(a, lo, preferred_element_type=F32))


def _dot_x2(a, b_exact):
    hi = a.astype(BF16)
    lo = (a - hi.astype(F32)).astype(BF16)
    b = b_exact.astype(BF16)
    return jnp.dot(hi, b, preferred_element_type=F32) + jnp.dot(lo, b, preferred_element_type=F32)


_SB_SUFFIX_DOT = _dot


def _dot_hi(a, b):
    ah = a.astype(BF16)
    al = (a - ah.astype(F32)).astype(BF16)
    bh = b.astype(BF16)
    bl = (b - bh.astype(F32)).astype(BF16)
    return (jnp.dot(ah, bh, preferred_element_type=F32)
            + jnp.dot(al, bh, preferred_element_type=F32)
            + jnp.dot(ah, bl, preferred_element_type=F32))


def _neg_abs(x):
    bits = lax.bitcast_convert_type(x, jnp.int32) | jnp.int32(-2 ** 31)
    return lax.bitcast_convert_type(bits, F32)


def _softplus(x):
    return jnp.maximum(x, 0.0) + jnp.log1p(jnp.exp(-jnp.abs(x)))


def _rmsnorm_kernel(x_ref, g_ref, o_ref):
    x = x_ref[...]
    ms = jnp.mean(x * x, axis=-1, keepdims=True)
    o_ref[...] = (x * lax.rsqrt(ms + NORM_EPS) * g_ref[...]).astype(o_ref.dtype)


def _rmsnorm(x, g, tm=256):
    m, d = x.shape
    return pl.pallas_call(
        _rmsnorm_kernel,
        grid=(m // tm,),
        in_specs=[pl.BlockSpec((tm, d), lambda i: (i, 0)),
                  pl.BlockSpec((1, d), lambda i: (0, 0))],
        out_specs=pl.BlockSpec((tm, d), lambda i: (i, 0)),
        out_shape=jax.ShapeDtypeStruct((m, d), BF16),
        compiler_params=_cparams(("parallel",)),
        name="rmsnorm",
    )(x, g.reshape(1, d))


def _mm_kernel(*refs, n_lhs, has_res):
    o_ref = refs[-1]
    acc = None
    for a_ref, w_ref in zip(refs[:n_lhs], refs[n_lhs:2 * n_lhs]):
        d = jnp.dot(a_ref[...], w_ref[...], preferred_element_type=F32)
        acc = d if acc is None else acc + d
    if has_res:
        acc = acc + refs[2 * n_lhs][...]
    o_ref[...] = acc.astype(o_ref.dtype)


def _matmul(lhs_list, w, residual=None, tm=1024, tn=512, name="matmul"):
    m = lhs_list[0].shape[0]
    n = w.shape[1]
    in_specs, args = [], []
    for a in lhs_list:
        in_specs.append(pl.BlockSpec((tm, a.shape[1]), lambda i, j: (i, 0)))
        args.append(a)
    row = 0
    for a in lhs_list:
        k = a.shape[1]
        assert row % k == 0
        in_specs.append(pl.BlockSpec((k, tn), functools.partial(lambda i, j, rb: (rb, j), rb=row // k)))
        args.append(w)
        row += k
    assert row == w.shape[0]
    if residual is not None:
        in_specs.append(pl.BlockSpec((tm, tn), lambda i, j: (i, j)))
        args.append(residual)
    return pl.pallas_call(
        functools.partial(_mm_kernel, n_lhs=len(lhs_list), has_res=residual is not None),
        grid=(m // tm, n // tn),
        in_specs=in_specs,
        out_specs=pl.BlockSpec((tm, tn), lambda i, j: (i, j)),
        out_shape=jax.ShapeDtypeStruct((m, n), F32),
        compiler_params=_cparams(("parallel", "arbitrary")),
        name=name,
    )(*args)


FFN_HALO = 16


def _ffn_up_kernel(h_ref, halo_ref, wg_ref, wv_ref, cg_ref, cv_ref, o_ref, sg, sv, *, tm, seq):
    i = pl.program_id(0)
    first = (i * tm) % seq == 0
    h = h_ref[...]
    halo = jnp.where(first, jnp.zeros_like(halo_ref[...]), halo_ref[...])

    def conv(w_ref, c_ref, s):
        s[0:FFN_HALO, :] = jnp.dot(halo, w_ref[...], preferred_element_type=F32)
        s[FFN_HALO:FFN_HALO + tm, :] = jnp.dot(h, w_ref[...], preferred_element_type=F32)
        acc = None
        for k in range(FFN_CONV_K):
            off = FFN_HALO - (FFN_CONV_K - 1) + k
            term = c_ref[k:k + 1, :] * s[off:off + tm, :]
            acc = term if acc is None else acc + term
        return acc

    g = conv(wg_ref, cg_ref, sg)
    v = conv(wv_ref, cv_ref, sv)
    o_ref[...] = (g * jax.nn.sigmoid(g) * v).astype(o_ref.dtype)


def _ffn_up(h, w_up, conv, seq, tm=1024, tn=512):
    m, d = h.shape
    nj = D_FF // tn
    return pl.pallas_call(
        functools.partial(_ffn_up_kernel, tm=tm, seq=seq),
        grid=(m // tm, nj),
        in_specs=[
            pl.BlockSpec((tm, d), lambda i, j: (i, 0)),
            pl.BlockSpec((FFN_HALO, d), lambda i, j: (jnp.maximum(i * (tm // FFN_HALO) - 1, 0), 0)),
            pl.BlockSpec((d, tn), lambda i, j: (0, j)),
            pl.BlockSpec((d, tn), lambda i, j: (0, j + nj)),
            pl.BlockSpec((FFN_CONV_K, tn), lambda i, j: (0, j)),
            pl.BlockSpec((FFN_CONV_K, tn), lambda i, j: (0, j + nj)),
        ],
        out_specs=pl.BlockSpec((tm, tn), lambda i, j: (i, j)),
        out_shape=jax.ShapeDtypeStruct((m, D_FF), BF16),
        scratch_shapes=[pltpu.VMEM((FFN_HALO + tm, tn), F32),
                        pltpu.VMEM((FFN_HALO + tm, tn), F32)],
        compiler_params=_cparams(("parallel", "arbitrary")),
        name="ffn_up_conv_gate",
    )(h, h, w_up, w_up, conv, conv)


CONF_HALO = 32
CONF_ROWS = 32
CONF_COLS = 256


def _conformer_kernel(val_ref, gate_ref, hval_ref, hgate_ref, cw_ref, cb_ref, lng_ref, lnb_ref,
                      o_ref, glu_s, u_s, *, tm, seq):
    i = pl.program_id(0)
    first = (i * tm) % seq == 0
    hglu = hval_ref[...] * jax.nn.sigmoid(hgate_ref[...])
    glu_s[0:CONF_HALO, :] = jnp.where(first, jnp.zeros_like(hglu), hglu)
    glu_s[CONF_HALO:CONF_HALO + tm, :] = val_ref[...] * jax.nn.sigmoid(gate_ref[...])
    base = CONF_HALO - (CONV_K - 1)
    for r0 in range(0, tm, CONF_ROWS):
        for c0 in range(0, CONV_W, CONF_COLS):
            acc = jnp.zeros((CONF_ROWS, CONF_COLS), F32) + cb_ref[:, c0:c0 + CONF_COLS]
            for k in range(CONV_K):
                acc = acc + cw_ref[k:k + 1, c0:c0 + CONF_COLS] * glu_s[r0 + base + k:r0 + base + k + CONF_ROWS,
                                                                   c0:c0 + CONF_COLS]
            u_s[r0:r0 + CONF_ROWS, c0:c0 + CONF_COLS] = acc
    u = u_s[...]
    mean = jnp.mean(u, axis=-1, keepdims=True)
    d = u - mean
    var = jnp.mean(d * d, axis=-1, keepdims=True)
    y = d * lax.rsqrt(var + LN_EPS) * lng_ref[...] + lnb_ref[...]
    o_ref[...] = (y * jax.nn.sigmoid(y)).astype(o_ref.dtype)


def _conformer(p, cw, cb, lng, lnb, seq, tm=256):
    m = p.shape[0]
    vb, gb = COL_CVAL // CONV_W, COL_CGATE // CONV_W
    hmap = lambda i: jnp.maximum(i * (tm // CONF_HALO) - 1, 0)
    return pl.pallas_call(
        functools.partial(_conformer_kernel, tm=tm, seq=seq),
        grid=(m // tm,),
        in_specs=[
            pl.BlockSpec((tm, CONV_W), lambda i: (i, vb)),
            pl.BlockSpec((tm, CONV_W), lambda i: (i, gb)),
            pl.BlockSpec((CONF_HALO, CONV_W), lambda i: (hmap(i), vb)),
            pl.BlockSpec((CONF_HALO, CONV_W), lambda i: (hmap(i), gb)),
            pl.BlockSpec((CONV_K, CONV_W), lambda i: (0, 0)),
            pl.BlockSpec((1, CONV_W), lambda i: (0, 0)),
            pl.BlockSpec((1, CONV_W), lambda i: (0, 0)),
            pl.BlockSpec((1, CONV_W), lambda i: (0, 0)),
        ],
        out_specs=pl.BlockSpec((tm, CONV_W), lambda i: (i, 0)),
        out_shape=jax.ShapeDtypeStruct((m, CONV_W), BF16),
        scratch_shapes=[pltpu.VMEM((CONF_HALO + tm, CONV_W), F32),
                        pltpu.VMEM((tm, CONV_W), F32)],
        compiler_params=_cparams(("parallel",)),
        name="conformer_conv",
    )(p, p, p, p, cw, cb.reshape(1, -1), lng.reshape(1, -1), lnb.reshape(1, -1))


SB_BQ = 512
SB_BK = 256
SB_PREP_ROWS = 512
LOG2E = 1.4426950408889634
LN2 = 0.6931471805599453


def _sb_kernel(q_ref, k_ref, v_ref, qg_ref, kg_ref, o_ref, kn_s, vb_s, *, seq):
    qi = pl.program_id(2)

    @pl.when(qi == 0)
    def _():
        def body(c, carry):
            sl = pl.ds(pl.multiple_of(c * SB_PREP_ROWS, SB_PREP_ROWS), SB_PREP_ROWS)
            k = k_ref[sl, :]
            ms = jnp.mean(k * k, axis=-1, keepdims=True)
            kn_s[sl, :] = (k * lax.rsqrt(ms + NORM_EPS) * kg_ref[...]).astype(BF16)
            vb_s[sl, :] = v_ref[sl, :].astype(BF16)
            return carry
        lax.fori_loop(0, seq // SB_PREP_ROWS, body, 0)

    q = q_ref[...]
    ms = jnp.mean(q * q, axis=-1, keepdims=True)
    q = (q * lax.rsqrt(ms + NORM_EPS) * qg_ref[...] * (SB_HEAD ** -0.5 * LOG2E)).astype(BF16)

    row = lax.broadcasted_iota(jnp.int32, (SB_BK, SB_BK), 0)
    col = lax.broadcasted_iota(jnp.int32, (SB_BK, SB_BK), 1)
    later = (row > col).astype(BF16)
    causal = col < row

    def blocks(qs, k0s, nsub, cs, accs, masked):
        n = range(len(qs))
        sls = [pl.ds(pl.multiple_of(k0, SB_BK), nsub * SB_BK) for k0 in k0s]
        zs = [lax.dot_general(qs[i], kn_s[sls[i], :], (((1,), (1,)), ((), ())), preferred_element_type=F32)
              for i in n]
        sps = [jnp.maximum(z, 0.0) + jnp.log(1.0 + jnp.exp2(_neg_abs(z))) * (1.0 / LN2) for z in zs]
        if masked:
            sps = [jnp.where(causal, sp, 0.0) for sp in sps]
        atts = [[None] * nsub for _ in n]
        for j in reversed(range(nsub)):
            cols = slice(j * SB_BK, (j + 1) * SB_BK)
            sp_j = [sp[:, cols] for sp in sps]
            suffix = [_SB_SUFFIX_DOT(x, later) for x in sp_j]
            att = [jnp.exp2((zs[i][:, cols] - sp_j[i]) - (suffix[i] + cs[i])) for i in n]
            if masked:
                att = [jnp.where(causal, x, 0.0) for x in att]
            for i in n:
                atts[i][j] = att[i].astype(BF16)
            cs = [cs[i] + jnp.sum(sp_j[i], axis=-1, keepdims=True) for i in n]
        accs = [accs[i] + jnp.dot(atts[i][0] if nsub == 1 else jnp.concatenate(atts[i], axis=1),
                                  vb_s[sls[i], :], preferred_element_type=F32) for i in n]
        return cs, accs

    tiles = range(SB_BQ // SB_BK)
    qs = [q[h * SB_BK:(h + 1) * SB_BK, :] for h in tiles]
    cs = [jnp.zeros((SB_BK, 1), F32) for _ in tiles]
    accs = [jnp.zeros((SB_BK, SB_HEAD), F32) for _ in tiles]
    base = qi * SB_BQ
    cs, accs = blocks(qs, [base + h * SB_BK for h in tiles], 1, cs, accs, True)
    c1, a1 = blocks(qs[1:], [base], 1, cs[1:], accs[1:], False)
    cs, accs = [cs[0]] + c1, [accs[0]] + a1

    def body(n, carry):
        k0 = base - (n + 1) * 2 * SB_BK
        c, a = blocks(qs, [k0, k0], 2, list(carry[0]), list(carry[1]), False)
        return tuple(c), tuple(a)

    cs, accs = lax.fori_loop(0, qi, body, (tuple(cs), tuple(accs)))
    for h in tiles:
        o_ref[h * SB_BK:(h + 1) * SB_BK, :] = accs[h].astype(o_ref.dtype)


def _stick_breaking(p, qg, kg, bsz, seq):
    assert SB_BQ == 2 * SB_BK
    nq = seq // SB_BQ
    qb, kb, vb = COL_SBQ // SB_HEAD, COL_SBK // SB_HEAD, COL_SBV // SB_HEAD
    return pl.pallas_call(
        functools.partial(_sb_kernel, seq=seq),
        grid=(bsz, SB_HEADS, nq),
        in_specs=[
            pl.BlockSpec((SB_BQ, SB_HEAD), lambda b, h, q: (b * nq + q, qb + h)),
            pl.BlockSpec((seq, SB_HEAD), lambda b, h, q: (b, kb + h)),
            pl.BlockSpec((seq, SB_HEAD), lambda b, h, q: (b, vb + h)),
            pl.BlockSpec((1, SB_HEAD), lambda b, h, q: (0, 0)),
            pl.BlockSpec((1, SB_HEAD), lambda b, h, q: (0, 0)),
        ],
        out_specs=pl.BlockSpec((SB_BQ, SB_HEAD), lambda b, h, q: (b * nq + q, h)),
        out_shape=jax.ShapeDtypeStruct((bsz * seq, SB_W), BF16),
        scratch_shapes=[pltpu.VMEM((seq, SB_HEAD), BF16), pltpu.VMEM((seq, SB_HEAD), BF16)],
        compiler_params=_cparams(("parallel", "parallel", "arbitrary")),
        name="stick_breaking",
    )(p, p, p, qg.reshape(1, -1), kg.reshape(1, -1))


PREP_HALO = 8


def _head_ones():
    r = lax.broadcasted_iota(jnp.int32, (PAIR, PAIR), 0) // RWKV_HEAD
    c = lax.broadcasted_iota(jnp.int32, (PAIR, PAIR), 1) // RWKV_HEAD
    return (r == c).astype(BF16)


def _rwkv_prep_kernel(p_ref, lora_ref, hp_ref, hlora_ref, mu_ref, mul_ref, w0_ref, wup_ref, a0_ref,
                      aup_ref, gup_ref, kk_ref, ka_ref, rk_ref,
                      r_o, k_o, v_o, lw_o, a_o, b_o, bonus_o, g_o, *, tm, seq):
    i = pl.program_id(0)
    first = (i * tm) % seq == 0

    def shifted(x_ref, h_ref, m_ref, c0, c1):
        x = x_ref[:, c0:c1]
        prev = pltpu.roll(x, 1, 0)
        row = lax.broadcasted_iota(jnp.int32, x.shape, 0)
        last = h_ref[PREP_HALO - 1:PREP_HALO, c0:c1]
        last = jnp.where(first, jnp.zeros_like(last), last)
        prev = jnp.where(row == 0, last, prev)
        return x + m_ref[:, c0:c1] * (prev - x)

    lo = shifted(lora_ref, hlora_ref, mul_ref, 0, LORA_COLS)
    w_lo = lo[:, 0:LORA_PAD]
    a_lo = lo[:, LORA_PAD:2 * LORA_PAD]
    g_lo = lo[:, 2 * LORA_PAD:]
    ones = _head_ones()

    for c0 in range(0, RWKV_W, 512):
        c1 = c0 + 512
        r = shifted(p_ref, hp_ref, mu_ref, c0, c1)
        k = shifted(p_ref, hp_ref, mu_ref, RWKV_W + c0, RWKV_W + c1)
        v = shifted(p_ref, hp_ref, mu_ref, 2 * RWKV_W + c0, 2 * RWKV_W + c1)
        y = w0_ref[:, c0:c1] + _dot(jnp.tanh(w_lo), wup_ref[:, c0:c1])
        log_w = -_softplus(-y) - 0.5
        lw_o[:, c0:c1] = -jnp.exp(log_w)
        iclr = jax.nn.sigmoid(a0_ref[:, c0:c1] + _dot(a_lo, aup_ref[:, c0:c1]))
        g_o[:, c0:c1] = _dot(jax.nn.sigmoid(g_lo), gup_ref[:, c0:c1])
        kk = k * kk_ref[:, c0:c1]
        kt = k * (1.0 + (iclr - 1.0) * ka_ref[:, c0:c1])
        rkr = r * kt * rk_ref[:, c0:c1]
        for t0 in range(0, 512, PAIR):
            t1 = t0 + PAIR
            kk_t = kk[:, t0:t1]
            ss = _dot_x3(kk_t * kk_t, ones)
            kk_n = kk_t * lax.rsqrt(jnp.maximum(ss, 1e-24))
            a_o[:, c0 + t0:c0 + t1] = -kk_n
            b_o[:, c0 + t0:c0 + t1] = kk_n * iclr[:, t0:t1]
            bonus_o[:, c0 + t0:c0 + t1] = _dot_x3(rkr[:, t0:t1], ones) * v[:, t0:t1]
        r_o[:, c0:c1] = r
        k_o[:, c0:c1] = kt
        v_o[:, c0:c1] = v


def _rwkv_prep(p, mu, mul, w0, wup, a0, aup, gup, kk, ka, rk, seq, tm=128):
    m = p.shape[0]
    lb = COL_LORA // LORA_COLS
    hmap = lambda i: jnp.maximum(i * (tm // PREP_HALO) - 1, 0)
    row = lambda a: a.reshape(1, -1)
    vec = pl.BlockSpec((1, RWKV_W), lambda i: (0, 0))
    out = jax.ShapeDtypeStruct((m, RWKV_W), F32)
    ospec = pl.BlockSpec((tm, RWKV_W), lambda i: (i, 0))
    return pl.pallas_call(
        functools.partial(_rwkv_prep_kernel, tm=tm, seq=seq),
        grid=(m // tm,),
        in_specs=[
            pl.BlockSpec((tm, 3 * RWKV_W), lambda i: (i, 0)),
            pl.BlockSpec((tm, LORA_COLS), lambda i: (i, lb)),
            pl.BlockSpec((PREP_HALO, 3 * RWKV_W), lambda i: (hmap(i), 0)),
            pl.BlockSpec((PREP_HALO, LORA_COLS), lambda i: (hmap(i), lb)),
            pl.BlockSpec((1, 3 * RWKV_W), lambda i: (0, 0)),
            pl.BlockSpec((1, LORA_COLS), lambda i: (0, 0)),
            vec,
            pl.BlockSpec((LORA_PAD, RWKV_W), lambda i: (0, 0)),
            vec,
            pl.BlockSpec((LORA_PAD, RWKV_W), lambda i: (0, 0)),
            pl.BlockSpec((GATE_LORA, RWKV_W), lambda i: (0, 0)),
            vec, vec, vec,
        ],
        out_specs=[ospec] * 8,
        out_shape=[out] * 8,
        compiler_params=_cparams(("parallel",)),
        name="rwkv_prep",
    )(p, p, p, p, row(mu), row(mul), row(w0), wup, row(a0), aup, gup, row(kk), row(ka), row(rk))


SCAN_PAIRS = 2
GN_ROWS = 256


def _rwkv_scan_kernel(r_ref, k_ref, v_ref, lw_ref, a_ref, b_ref, bonus_ref, g_ref, gng_ref, gnb_ref,
                      o_ref, s_ref, rq_s, oin_s, tr_s, dl_s, *, tblk):
    C = RWKV_C
    t_id = pl.program_id(2)

    @pl.when(t_id == 0)
    def _():
        s_ref[...] = jnp.zeros_like(s_ref)

    lane = lax.broadcasted_iota(jnp.int32, (1, PAIR), 1)
    m0 = (lane < RWKV_HEAD).astype(F32)
    m1 = 1.0 - m0
    ti = lax.broadcasted_iota(jnp.int32, (C, C), 0)
    ii = lax.broadcasted_iota(jnp.int32, (C, C), 1)
    tri_incl = (ii <= ti).astype(BF16)
    t2 = lax.broadcasted_iota(jnp.int32, (C, 2 * C), 0)
    i2 = lax.broadcasted_iota(jnp.int32, (C, 2 * C), 1) % C
    strict2 = (i2 < t2).astype(F32)
    incl2 = (i2 <= t2).astype(F32)
    rr = lax.broadcasted_iota(jnp.int32, (PAIR, PAIR), 0)
    cc = lax.broadcasted_iota(jnp.int32, (PAIR, PAIR), 1)
    bdmask = (rr // RWKV_HEAD == cc // RWKV_HEAD).astype(F32)
    eye = (rr == cc).astype(F32)
    ones = _head_ones()

    def stack_heads(x):
        return jnp.concatenate([x * m0, x * m1], axis=0)

    def chunk_pair(c, pi):
        sl = pl.ds(pl.multiple_of(c * C, C), C)
        ln = slice(pi * PAIR, (pi + 1) * PAIR)
        lw = lw_ref[sl, ln]
        r, k, v, a, b = r_ref[sl, ln], k_ref[sl, ln], v_ref[sl, ln], a_ref[sl, ln], b_ref[sl, ln]
        cum = _x3_dot(tri_incl, lw)
        tot = cum[C - 1:C, :]
        e_neg = jnp.exp(-cum)
        e_end = jnp.exp(tot - cum)
        at = a * jnp.exp(cum - lw)
        rt = r * jnp.exp(cum)
        bh, kh = b * e_neg, k * e_neg
        bend, kend = b * e_end, k * e_end
        gram = _dot_nt(jnp.concatenate([at, rt], axis=0),
                       jnp.concatenate([stack_heads(bh), stack_heads(kh)], axis=0))
        m_ab = gram[0:C, 0:2 * C] * strict2
        m_ak = gram[0:C, 2 * C:] * strict2
        q_b = gram[C:, 0:2 * C] * incl2
        q_k = gram[C:, 2 * C:] * incl2
        vv = stack_heads(v)
        u_rhs = _dot(m_ak, vv)
        bd = stack_heads(m_ab)
        t_inv = eye + bd
        pw = bd
        for _ in range(5):
            pw = _dot(pw, pw)
            t_inv = t_inv + _dot(t_inv, pw)
        t_ls = t_inv[0:C, :] + t_inv[C:, :]
        sol = _dot(t_ls, jnp.concatenate(
            [jnp.concatenate([at * m0, u_rhs * m0], axis=1),
             jnp.concatenate([at * m1, u_rhs * m1], axis=1)], axis=0))
        wa, u0 = sol[:, 0:PAIR], sol[:, PAIR:]
        y = _dot(q_b, jnp.concatenate(
            [jnp.concatenate([wa * m0, u0 * m0], axis=1),
             jnp.concatenate([wa * m1, u0 * m1], axis=1)], axis=0))
        rq_s[sl, ln] = rt + y[:, 0:PAIR]
        oin_s[sl, ln] = y[:, PAIR:] + _dot(q_k, vv)
        tr_s[pi, c] = _dot_tn(wa, bend) * bdmask + eye * jnp.exp(tot)
        dl_s[pi, c] = (_dot_tn(u0, bend) + _dot_tn(v, kend)) * bdmask

    def chunk(c, carry):
        for pi in range(SCAN_PAIRS):
            chunk_pair(c, pi)
        return carry

    lax.fori_loop(0, tblk // C, chunk, 0)

    def scan(c, carry):
        sl = pl.ds(pl.multiple_of(c * C, C), C)
        for pi in range(SCAN_PAIRS):
            ln = slice(pi * PAIR, (pi + 1) * PAIR)
            s = s_ref[pi]
            oin_s[sl, ln] = _dot_nt(rq_s[sl, ln], s) + oin_s[sl, ln]
            s_ref[pi] = _dot(s, tr_s[pi, c]) + dl_s[pi, c]
        return carry

    lax.fori_loop(0, tblk // C, scan, 0)

    for pi in range(SCAN_PAIRS):
        ln = slice(pi * PAIR, (pi + 1) * PAIR)
        for r0 in range(0, tblk, GN_ROWS):
            rows = slice(r0, r0 + GN_ROWS)
            o = oin_s[rows, ln]
            mean = _dot_x3(o, ones) * (1.0 / RWKV_HEAD)
            d = o - mean
            var = _dot_x3(d * d, ones) * (1.0 / RWKV_HEAD)
            on = d * lax.rsqrt(var + GN_EPS) * gng_ref[:, ln] + gnb_ref[:, ln]
            o_ref[rows, ln] = ((on + bonus_ref[rows, ln]) * g_ref[rows, ln]).astype(o_ref.dtype)


def _rwkv_scan(r, k, v, lw, a, b, bonus, g, gng, gnb, bsz, seq, tblk=512):
    nt = seq // tblk
    width = SCAN_PAIRS * PAIR
    tile = pl.BlockSpec((tblk, width), lambda bi, h, t: (bi * nt + t, h))
    vec = pl.BlockSpec((1, width), lambda bi, h, t: (0, h))
    nch = tblk // RWKV_C
    return pl.pallas_call(
        functools.partial(_rwkv_scan_kernel, tblk=tblk),
        grid=(bsz, RWKV_W // width, nt),
        in_specs=[tile] * 8 + [vec, vec],
        out_specs=tile,
        out_shape=jax.ShapeDtypeStruct((bsz * seq, RWKV_W), BF16),
        scratch_shapes=[pltpu.VMEM((SCAN_PAIRS, PAIR, PAIR), F32),
                        pltpu.VMEM((tblk, width), F32), pltpu.VMEM((tblk, width), F32),
                        pltpu.VMEM((SCAN_PAIRS, nch, PAIR, PAIR), F32),
                        pltpu.VMEM((SCAN_PAIRS, nch, PAIR, PAIR), F32)],
        compiler_params=_cparams(("parallel", "parallel", "arbitrary")),
        name="rwkv_scan",
    )(r, k, v, lw, a, b, bonus, g, gng.reshape(1, -1), gnb.reshape(1, -1))


QUAD_HEADS = 2
QUAD = QUAD_HEADS * RWKV_HEAD
SCAN_GROUPS = 8


def _rwkv_quad_kernel(r_ref, k_ref, v_ref, lw_ref, a_ref, b_ref, bonus_ref, g_ref, gng_ref, gnb_ref,
                      o_ref, s_ref, rq_s, oin_s, tr_s, dl_s, *, tblk):
    C = RWKV_C
    HC = QUAD_HEADS * C
    G = range(SCAN_GROUPS)
    t_id = pl.program_id(2)

    @pl.when(t_id == 0)
    def _():
        s_ref[...] = jnp.zeros_like(s_ref)

    lane = lax.broadcasted_iota(jnp.int32, (1, QUAD), 1) // RWKV_HEAD
    hm = [(lane == h).astype(F32) for h in range(QUAD_HEADS)]
    ti = lax.broadcasted_iota(jnp.int32, (C, C), 0)
    ii = lax.broadcasted_iota(jnp.int32, (C, C), 1)
    tri_incl = (ii <= ti).astype(BF16)
    t4 = lax.broadcasted_iota(jnp.int32, (C, HC), 0)
    i4 = lax.broadcasted_iota(jnp.int32, (C, HC), 1) % C
    strict4 = (i4 < t4).astype(F32)
    incl4 = (i4 <= t4).astype(F32)
    rr = lax.broadcasted_iota(jnp.int32, (QUAD, QUAD), 0)
    cc = lax.broadcasted_iota(jnp.int32, (QUAD, QUAD), 1)
    bdmask = (rr // RWKV_HEAD == cc // RWKV_HEAD).astype(F32)
    eye = (rr == cc).astype(F32)
    ones = (rr // RWKV_HEAD == cc // RWKV_HEAD).astype(BF16)

    def stack_heads(x):
        return jnp.concatenate([x * m for m in hm], axis=0)

    def stack_heads2(x, y):
        return jnp.concatenate([jnp.concatenate([x * m, y * m], axis=1) for m in hm], axis=0)

    def chunk(c, carry):
        sl = pl.ds(pl.multiple_of(c * C, C), C)
        lns = [slice(g * QUAD, (g + 1) * QUAD) for g in G]
        lw = [lw_ref[sl, ln] for ln in lns]
        cum = [_x3_dot(tri_incl, x) for x in lw]
        tot = [x[C - 1:C, :] for x in cum]
        e_neg = [jnp.exp(-x) for x in cum]
        e_end = [jnp.exp(t - x) for t, x in zip(tot, cum)]
        at = [a_ref[sl, ln] * jnp.exp(x - w) for ln, x, w in zip(lns, cum, lw)]
        rt = [r_ref[sl, ln] * jnp.exp(x) for ln, x in zip(lns, cum)]
        b = [b_ref[sl, ln] for ln in lns]
        k = [k_ref[sl, ln] for ln in lns]
        v = [v_ref[sl, ln] for ln in lns]
        gram = [_dot_nt(jnp.concatenate([at[g], rt[g]], axis=0),
                        jnp.concatenate([stack_heads(b[g] * e_neg[g]), stack_heads(k[g] * e_neg[g])], axis=0))
                for g in G]
        m_ak = [x[0:C, HC:] * strict4 for x in gram]
        q_b = [x[C:, 0:HC] * incl4 for x in gram]
        q_k = [x[C:, HC:] * incl4 for x in gram]
        vv = [stack_heads(x) for x in v]
        u_rhs = [_dot(m_ak[g], vv[g]) for g in G]
        pw = [stack_heads(x[0:C, 0:HC] * strict4) for x in gram]
        t_inv = [eye + x for x in pw]
        pw = [_dot(x, x) for x in pw]
        for _ in range(4):
            both = [_dot(jnp.concatenate([x, t], axis=0), x) for t, x in zip(t_inv, pw)]
            pw = [x[0:HC, :] for x in both]
            t_inv = [t + x[HC:, :] for t, x in zip(t_inv, both)]
        t_inv = [t + _dot(t, x) for t, x in zip(t_inv, pw)]
        t_ls = [sum(t[h * C:(h + 1) * C, :] for h in range(1, QUAD_HEADS)) + t[0:C, :] for t in t_inv]
        sol = [_dot(t_ls[g], stack_heads2(at[g], u_rhs[g])) for g in G]
        wa = [x[:, 0:QUAD] for x in sol]
        u0 = [x[:, QUAD:] for x in sol]
        y = [_dot(q_b[g], stack_heads2(wa[g], u0[g])) for g in G]
        o_in2 = [_dot(q_k[g], vv[g]) for g in G]
        bend = [b[g] * e_end[g] for g in G]
        tr = [_dot_tn(wa[g], bend[g]) for g in G]
        dl = [_dot_tn(jnp.concatenate([u0[g], v[g]], axis=0),
                      jnp.concatenate([bend[g], k[g] * e_end[g]], axis=0)) for g in G]
        for g in G:
            rq_s[sl, lns[g]] = rt[g] + y[g][:, 0:QUAD]
            oin_s[sl, lns[g]] = y[g][:, QUAD:] + o_in2[g]
            tr_s[g, c] = tr[g] * bdmask + eye * jnp.exp(tot[g])
            dl_s[g, c] = dl[g] * bdmask
        return carry

    lax.fori_loop(0, tblk // C, chunk, 0)

    def scan(c, carry):
        sl = pl.ds(pl.multiple_of(c * C, C), C)
        for g in G:
            ln = slice(g * QUAD, (g + 1) * QUAD)
            s = s_ref[g]
            oin_s[sl, ln] = _dot_nt(rq_s[sl, ln], s) + oin_s[sl, ln]
            s_ref[g] = _dot(s, tr_s[g, c]) + dl_s[g, c]
        return carry

    lax.fori_loop(0, tblk // C, scan, 0)

    for g in G:
        ln = slice(g * QUAD, (g + 1) * QUAD)
        for r0 in range(0, tblk, GN_ROWS):
            rows = slice(r0, r0 + GN_ROWS)
            o = oin_s[rows, ln]
            mean = _dot_x3(o, ones) * (1.0 / RWKV_HEAD)
            d = o - mean
            var = _dot_x3(d * d, ones) * (1.0 / RWKV_HEAD)
            on = d * lax.rsqrt(var + GN_EPS) * gng_ref[:, ln] + gnb_ref[:, ln]
            o_ref[rows, ln] = ((on + bonus_ref[rows, ln]) * g_ref[rows, ln]).astype(o_ref.dtype)


def _rwkv_quad_scan(r, k, v, lw, a, b, bonus, g, gng, gnb, bsz, seq, tblk=512):
    nt = seq // tblk
    width = SCAN_GROUPS * QUAD
    tile = pl.BlockSpec((tblk, width), lambda bi, h, t: (bi * nt + t, h))
    vec = pl.BlockSpec((1, width), lambda bi, h, t: (0, h))
    nch = tblk // RWKV_C
    return pl.pallas_call(
        functools.partial(_rwkv_quad_kernel, tblk=tblk),
        grid=(bsz, RWKV_W // width, nt),
        in_specs=[tile] * 8 + [vec, vec],
        out_specs=tile,
        out_shape=jax.ShapeDtypeStruct((bsz * seq, RWKV_W), BF16),
        scratch_shapes=[pltpu.VMEM((SCAN_GROUPS, QUAD, QUAD), F32),
                        pltpu.VMEM((tblk, width), F32), pltpu.VMEM((tblk, width), F32),
                        pltpu.VMEM((SCAN_GROUPS, nch, QUAD, QUAD), F32),
                        pltpu.VMEM((SCAN_GROUPS, nch, QUAD, QUAD), F32)],
        compiler_params=_cparams(("parallel", "parallel", "arbitrary")),
        name="rwkv_scan",
    )(r, k, v, lw, a, b, bonus, g, gng.reshape(1, -1), gnb.reshape(1, -1))


def _relayout_cols(w, axis):
    def sl(a, b):
        idx = [slice(None)] * w.ndim
        idx[axis] = slice(a, b)
        return w[tuple(idx)]
    pad_shape = list(w.shape)
    pad_shape[axis] = LORA_PAD - DECAY_LORA
    z = jnp.zeros(pad_shape, w.dtype)
    lo = 3 * RWKV_W
    rest = lo + DECAY_LORA + ICLR_LORA + GATE_LORA
    return jnp.concatenate(
        [sl(0, lo), sl(rest, w.shape[axis]), sl(lo, lo + DECAY_LORA), z,
         sl(lo + DECAY_LORA, lo + DECAY_LORA + ICLR_LORA), z,
         sl(lo + DECAY_LORA + ICLR_LORA, rest)], axis=axis)


def _pad_rows(w):
    return jnp.concatenate([w, jnp.zeros((LORA_PAD - w.shape[0], w.shape[1]), w.dtype)], axis=0)


def _layer(x, bsz, seq, w_in, w_out, attn_norm, ffn_norm, mu, w0, w_up, a0, a_up, g_up, k_k, k_a, r_k,
           gn_g, gn_b, conv_w, conv_b, ln_g, ln_b, q_norm, k_norm, ffn_up, ffn_conv, ffn_down, tm_mm=1024):
    h = _rmsnorm(x, attn_norm)
    p = _matmul([h], w_in, tm=tm_mm, name="proj_in")
    mu_p, mu_l = mu[:3 * RWKV_W], mu[COL_LORA:]
    r, kt, v, lw, a, b, bonus, g = _rwkv_prep(p, mu_p, mu_l, w0, w_up, a0, a_up, g_up, k_k, k_a,
                                              r_k.reshape(-1), seq)
    y_a = _rwkv_quad_scan(r, kt, v, lw, a, b, bonus, g, gn_g, gn_b, bsz, seq, tblk=min(512, seq))
    y_b = _conformer(p, conv_w, conv_b, ln_g, ln_b, seq)
    y_c = _stick_breaking(p, q_norm, k_norm, bsz, seq)
    x = _matmul([y_a, y_b, y_c], w_out, residual=x, tm=tm_mm, name="proj_out")
    h = _rmsnorm(x, ffn_norm)
    act = _ffn_up(h, ffn_up, ffn_conv, seq, tm=tm_mm)
    return _matmul([act], ffn_down, residual=x, tm=min(512, tm_mm), name="ffn_down")


def kernel(x, w_in, w_out, attn_norm, ffn_norm, rwkv_mu, rwkv_w0, rwkv_w_up, rwkv_a0, rwkv_a_up,
           rwkv_g_up, rwkv_k_k, rwkv_k_a, rwkv_r_k, rwkv_gn_g, rwkv_gn_b, conv_w, conv_b, conv_ln_g,
           conv_ln_b, sb_q_norm, sb_k_norm, ffn_up, ffn_conv, ffn_down):
    bsz, seq, d = x.shape
    x = x.reshape(bsz * seq, d)
    for l in range(w_in.shape[0]):
        mu_full = jnp.concatenate([rwkv_mu[l], jnp.zeros((w_in.shape[2] - rwkv_mu.shape[1],), F32)])
        mu_re = _relayout_cols(mu_full, 0)
        x = _layer(
            x, bsz, seq,
            _relayout_cols(w_in[l], 1).astype(BF16), w_out[l].astype(BF16), attn_norm[l], ffn_norm[l],
            mu_re, rwkv_w0[l], _pad_rows(rwkv_w_up[l]).astype(BF16), rwkv_a0[l],
            _pad_rows(rwkv_a_up[l]).astype(BF16), rwkv_g_up[l].astype(BF16), rwkv_k_k[l], rwkv_k_a[l],
            rwkv_r_k[l], rwkv_gn_g[l], rwkv_gn_b[l], conv_w[l], conv_b[l], conv_ln_g[l], conv_ln_b[l],
            sb_q_norm[l], sb_k_norm[l], ffn_up[l].astype(BF16), ffn_conv[l], ffn_down[l].astype(BF16),
            tm_mm=min(1024, seq))
    return x.reshape(bsz, seq, d)
```

```python
import functools

import jax
import jax.numpy as jnp
from jax import lax
from jax.experimental import pallas as pl
from jax.experimental.pallas import tpu as pltpu

F32 = jnp.float32
BF16 = jnp.bfloat16

D_MODEL = 4096
DEPTH = 4
RWKV_HEAD = 64
RWKV_W = 2048
RWKV_HEADS = RWKV_W // RWKV_HEAD
DECAY_LORA = 96
ICLR_LORA = 96
GATE_LORA = 256
CONV_W = 1024
CONV_K = 31
SB_HEAD = 128
SB_W = 1024
SB_HEADS = SB_W // SB_HEAD
D_FF = 2 * D_MODEL
FFN_CONV_K = 3
NORM_EPS = 1e-6
LN_EPS = 1e-5
GN_EPS = RWKV_HEAD * 1e-5

LANES = 128
SUBLANES = 8
LORA_PAD = 128
COL_RKV = 0
COL_CVAL = 3 * RWKV_W
COL_CGATE = COL_CVAL + CONV_W
COL_SBQ = COL_CGATE + CONV_W
COL_SBK = COL_SBQ + SB_W
COL_SBV = COL_SBK + SB_W
COL_LORA = COL_SBV + SB_W
LORA_COLS = 2 * LORA_PAD + GATE_LORA
PROJ_PAD = COL_LORA + LORA_COLS
VMEM_LIMIT = 56 * 1024 * 1024

RWKV_C = 64
PAIR = 2 * RWKV_HEAD


def _cparams(sem):
    return pltpu.CompilerParams(dimension_semantics=sem, vmem_limit_bytes=VMEM_LIMIT)


def _dot(a, b):
    return jnp.dot(a.astype(BF16), b.astype(BF16), preferred_element_type=F32)


def _dot_nt(a, b):
    return lax.dot_general(a.astype(BF16), b.astype(BF16), (((1,), (1,)), ((), ())),
                           preferred_element_type=F32)


def _dot_tn(a, b):
    return lax.dot_general(a.astype(BF16), b.astype(BF16), (((0,), (0,)), ((), ())),
                           preferred_element_type=F32)


def _split3(x):
    hi = x.astype(BF16)
    r1 = x - hi.astype(F32)
    mid = r1.astype(BF16)
    lo = (r1 - mid.astype(F32)).astype(BF16)
    return hi, mid, lo


def _dot_x3(a, b_exact):
    hi, mid, lo = _split3(a)
    b = b_exact.astype(BF16)
    return (jnp.dot(hi, b, preferred_element_type=F32)
            + jnp.dot(mid, b, preferred_element_type=F32)
            + jnp.dot(lo, b, preferred_element_type=F32))


def _x3_dot(a_exact, b):
    hi, mid, lo = _split3(b)
    a = a_exact.astype(BF16)
    return (jnp.dot(a, hi, preferred_element_type=F32)
            + jnp.dot(a, mid, preferred_element_type=F32)
            + jnp.dot(a, lo, preferred_element_type=F32))


def _dot_x2(a, b_exact):
    hi = a.astype(BF16)
    lo = (a - hi.astype(F32)).astype(BF16)
    b = b_exact.astype(BF16)
    return jnp.dot(hi, b, preferred_element_type=F32) + jnp.dot(lo, b, preferred_element_type=F32)


_SB_SUFFIX_DOT = _dot


def _dot_hi(a, b):
    ah = a.astype(BF16)
    al = (a - ah.astype(F32)).astype(BF16)
    bh = b.astype(BF16)
    bl = (b - bh.astype(F32)).astype(BF16)
    return (jnp.dot(ah, bh, preferred_element_type=F32)
            + jnp.dot(al, bh, preferred_element_type=F32)
            + jnp.dot(ah, bl, preferred_element_type=F32))


def _neg_abs(x):
    bits = lax.bitcast_convert_type(x, jnp.int32) | jnp.int32(-2 ** 31)
    return lax.bitcast_convert_type(bits, F32)


def _softplus(x):
    return jnp.maximum(x, 0.0) + jnp.log1p(jnp.exp(-jnp.abs(x)))


def _rmsnorm_kernel(x_ref, g_ref, o_ref):
    x = x_ref[...]
    ms = jnp.mean(x * x, axis=-1, keepdims=True)
    o_ref[...] = (x * lax.rsqrt(ms + NORM_EPS) * g_ref[...]).astype(o_ref.dtype)


def _rmsnorm(x, g, tm=256):
    m, d = x.shape
    return pl.pallas_call(
        _rmsnorm_kernel,
        grid=(m // tm,),
        in_specs=[pl.BlockSpec((tm, d), lambda i: (i, 0)),
                  pl.BlockSpec((1, d), lambda i: (0, 0))],
        out_specs=pl.BlockSpec((tm, d), lambda i: (i, 0)),
        out_shape=jax.ShapeDtypeStruct((m, d), BF16),
        compiler_params=_cparams(("parallel",)),
        name="rmsnorm",
    )(x, g.reshape(1, d))


def _mm_kernel(*refs, n_lhs, has_res):
    o_ref = refs[-1]
    acc = None
    for a_ref, w_ref in zip(refs[:n_lhs], refs[n_lhs:2 * n_lhs]):
        d = jnp.dot(a_ref[...], w_ref[...], preferred_element_type=F32)
        acc = d if acc is None else acc + d
    if has_res:
        acc = acc + refs[2 * n_lhs][...]
    o_ref[...] = acc.astype(o_ref.dtype)


def _matmul(lhs_list, w, layer, residual=None, tm=1024, tn=512, name="matmul"):
    m = lhs_list[0].shape[0]
    n = w.shape[2]
    in_specs, args = [], []
    for a in lhs_list:
        in_specs.append(pl.BlockSpec((tm, a.shape[1]), lambda i, j: (i, 0)))
        args.append(a)
    row = 0
    for a in lhs_list:
        k = a.shape[1]
        assert row % k == 0
        in_specs.append(pl.BlockSpec((pl.Squeezed(), k, tn),
                                     functools.partial(lambda i, j, rb: (layer, rb, j), rb=row // k)))
        args.append(w)
        row += k
    assert row == w.shape[1]
    if residual is not None:
        in_specs.append(pl.BlockSpec((tm, tn), lambda i, j: (i, j)))
        args.append(residual)
    return pl.pallas_call(
        functools.partial(_mm_kernel, n_lhs=len(lhs_list), has_res=residual is not None),
        grid=(m // tm, n // tn),
        in_specs=in_specs,
        out_specs=pl.BlockSpec((tm, tn), lambda i, j: (i, j)),
        out_shape=jax.ShapeDtypeStruct((m, n), F32),
        compiler_params=_cparams(("parallel", "arbitrary")),
        name=name,
    )(*args)


FFN_CARRY = 8
FFN_SUB = 256


def _ffn_up_kernel(h_ref, wg_ref, wv_ref, cg_ref, cv_ref, o_ref, *scratch, tm, tn, seq):
    i = pl.program_id(1)
    first = (i * tm) % seq == 0
    nsub = tn // FFN_SUB
    sg, sv = scratch[:nsub], scratch[nsub:]

    @pl.when(first)
    def _():
        for s in scratch:
            s[0:FFN_CARRY, :] = jnp.zeros((FFN_CARRY, FFN_SUB), F32)

    @pl.when(jnp.logical_not(first))
    def _():
        for s in scratch:
            s[0:FFN_CARRY, :] = s[tm:tm + FFN_CARRY, :]

    def up(w_ref, s, n):
        s[FFN_CARRY:FFN_CARRY + tm, :] = jnp.dot(h_ref[...], w_ref[:, n * FFN_SUB:(n + 1) * FFN_SUB],
                                                 preferred_element_type=F32)

    def conv(c_ref, s, n):
        acc = None
        for k in range(FFN_CONV_K):
            off = FFN_CARRY - (FFN_CONV_K - 1) + k
            term = c_ref[k:k + 1, n * FFN_SUB:(n + 1) * FFN_SUB] * s[off:off + tm, :]
            acc = term if acc is None else acc + term
        return acc

    def gate(n):
        g = conv(cg_ref, sg[n], n)
        v = conv(cv_ref, sv[n], n)
        o_ref[:, n * FFN_SUB:(n + 1) * FFN_SUB] = (g * jax.nn.sigmoid(g) * v).astype(o_ref.dtype)

    up(wg_ref, sg[0], 0)
    up(wv_ref, sv[0], 0)
    for n in range(1, nsub):
        up(wg_ref, sg[n], n)
        gate(n - 1)
        up(wv_ref, sv[n], n)
    gate(nsub - 1)


def _ffn_up(h, w_up, layer, conv, seq, tm=1024, tn=512):
    m, d = h.shape
    nj = D_FF // tn
    return pl.pallas_call(
        functools.partial(_ffn_up_kernel, tm=tm, tn=tn, seq=seq),
        grid=(nj, m // tm),
        in_specs=[
            pl.BlockSpec((tm, d), lambda j, i: (i, 0)),
            pl.BlockSpec((pl.Squeezed(), d, tn), lambda j, i: (layer, 0, j)),
            pl.BlockSpec((pl.Squeezed(), d, tn), lambda j, i: (layer, 0, j + nj)),
            pl.BlockSpec((FFN_CONV_K, tn), lambda j, i: (0, j)),
            pl.BlockSpec((FFN_CONV_K, tn), lambda j, i: (0, j + nj)),
        ],
        out_specs=pl.BlockSpec((tm, tn), lambda j, i: (i, j)),
        out_shape=jax.ShapeDtypeStruct((m, D_FF), BF16),
        scratch_shapes=[pltpu.VMEM((FFN_CARRY + tm, FFN_SUB), F32)] * (2 * (tn // FFN_SUB)),
        compiler_params=_cparams(("parallel", "arbitrary")),
        name="ffn_up_conv_gate",
    )(h, w_up, w_up, conv, conv)


CONF_HALO = 32
CONF_ROWS = 32
CONF_COLS = 256


def _conformer_kernel(val_ref, gate_ref, hval_ref, hgate_ref, cw_ref, cb_ref, lng_ref, lnb_ref,
                      o_ref, glu_s, u_s, *, tm, seq):
    i = pl.program_id(0)
    first = (i * tm) % seq == 0
    hglu = hval_ref[...] * jax.nn.sigmoid(hgate_ref[...])
    glu_s[0, 0:CONF_HALO, :] = jnp.where(first, jnp.zeros_like(hglu), hglu)
    glu_s[0, CONF_HALO:CONF_HALO + tm, :] = val_ref[...] * jax.nn.sigmoid(gate_ref[...])
    for b in range(1, SUBLANES):
        glu_s[b, SUBLANES:CONF_HALO + tm, :] = glu_s[0, SUBLANES - b:CONF_HALO + tm - b, :]
    for r0 in range(0, tm, CONF_ROWS):
        for c0 in range(0, CONV_W, CONF_COLS):
            acc = jnp.zeros((CONF_ROWS, CONF_COLS), F32) + cb_ref[:, c0:c0 + CONF_COLS]
            for k in range(CONV_K):
                a, b = divmod(CONV_K - 1 - k, SUBLANES)
                start = CONF_HALO + r0 - SUBLANES * a
                acc = acc + cw_ref[k:k + 1, c0:c0 + CONF_COLS] * glu_s[b, start:start + CONF_ROWS,
                                                                   c0:c0 + CONF_COLS]
            u_s[r0:r0 + CONF_ROWS, c0:c0 + CONF_COLS] = acc
    u = u_s[...]
    mean = jnp.mean(u, axis=-1, keepdims=True)
    d = u - mean
    var = jnp.mean(d * d, axis=-1, keepdims=True)
    y = d * lax.rsqrt(var + LN_EPS) * lng_ref[...] + lnb_ref[...]
    o_ref[...] = (y * jax.nn.sigmoid(y)).astype(o_ref.dtype)


def _conformer(p, cw, cb, lng, lnb, seq, tm=256):
    m = p.shape[0]
    vb, gb = COL_CVAL // CONV_W, COL_CGATE // CONV_W
    hmap = lambda i: jnp.maximum(i * (tm // CONF_HALO) - 1, 0)
    return pl.pallas_call(
        functools.partial(_conformer_kernel, tm=tm, seq=seq),
        grid=(m // tm,),
        in_specs=[
            pl.BlockSpec((tm, CONV_W), lambda i: (i, vb)),
            pl.BlockSpec((tm, CONV_W), lambda i: (i, gb)),
            pl.BlockSpec((CONF_HALO, CONV_W), lambda i: (hmap(i), vb)),
            pl.BlockSpec((CONF_HALO, CONV_W), lambda i: (hmap(i), gb)),
            pl.BlockSpec((CONV_K, CONV_W), lambda i: (0, 0)),
            pl.BlockSpec((1, CONV_W), lambda i: (0, 0)),
            pl.BlockSpec((1, CONV_W), lambda i: (0, 0)),
            pl.BlockSpec((1, CONV_W), lambda i: (0, 0)),
        ],
        out_specs=pl.BlockSpec((tm, CONV_W), lambda i: (i, 0)),
        out_shape=jax.ShapeDtypeStruct((m, CONV_W), BF16),
        scratch_shapes=[pltpu.VMEM((SUBLANES, CONF_HALO + tm, CONV_W), F32),
                        pltpu.VMEM((tm, CONV_W), F32)],
        compiler_params=_cparams(("parallel",)),
        name="conformer_conv",
    )(p, p, p, p, cw, cb.reshape(1, -1), lng.reshape(1, -1), lnb.reshape(1, -1))


SB_BQ = 512
SB_BK = 256
SB_PREP_ROWS = 512
LOG2E = 1.4426950408889634
LN2 = 0.6931471805599453


def _sb_kernel(q_ref, k_ref, v_ref, qg_ref, kg_ref, o_ref, kn_s, vb_s, *, seq):
    qi = pl.program_id(2)

    @pl.when(qi == 0)
    def _():
        def body(c, carry):
            sl = pl.ds(pl.multiple_of(c * SB_PREP_ROWS, SB_PREP_ROWS), SB_PREP_ROWS)
            k = k_ref[sl, :]
            ms = jnp.mean(k * k, axis=-1, keepdims=True)
            kn_s[sl, :] = (k * lax.rsqrt(ms + NORM_EPS) * kg_ref[...]).astype(BF16)
            vb_s[sl, :] = v_ref[sl, :].astype(BF16)
            return carry
        lax.fori_loop(0, seq // SB_PREP_ROWS, body, 0)

    q = q_ref[...]
    ms = jnp.mean(q * q, axis=-1, keepdims=True)
    q = (q * lax.rsqrt(ms + NORM_EPS) * qg_ref[...] * (SB_HEAD ** -0.5 * LOG2E)).astype(BF16)

    row = lax.broadcasted_iota(jnp.int32, (SB_BK, SB_BK), 0)
    col = lax.broadcasted_iota(jnp.int32, (SB_BK, SB_BK), 1)
    later = (row > col).astype(BF16)
    causal = col < row

    def blocks(qs, k0s, nsub, cs, accs, masked):
        n = range(len(qs))
        sls = [pl.ds(pl.multiple_of(k0, SB_BK), nsub * SB_BK) for k0 in k0s]
        zs = [lax.dot_general(qs[i], kn_s[sls[i], :], (((1,), (1,)), ((), ())), preferred_element_type=F32)
              for i in n]
        sps = [jnp.maximum(z, 0.0) + jnp.log(1.0 + jnp.exp2(_neg_abs(z))) * (1.0 / LN2) for z in zs]
        if masked:
            sps = [jnp.where(causal, sp, 0.0) for sp in sps]
        atts = [[None] * nsub for _ in n]
        for j in reversed(range(nsub)):
            cols = slice(j * SB_BK, (j + 1) * SB_BK)
            sp_j = [sp[:, cols] for sp in sps]
            suffix = [_SB_SUFFIX_DOT(x, later) for x in sp_j]
            att = [jnp.exp2((zs[i][:, cols] - sp_j[i]) - (suffix[i] + cs[i])) for i in n]
            if masked:
                att = [jnp.where(causal, x, 0.0) for x in att]
            for i in n:
                atts[i][j] = att[i].astype(BF16)
            cs = [cs[i] + jnp.sum(sp_j[i], axis=-1, keepdims=True) for i in n]
        accs = [accs[i] + jnp.dot(atts[i][0] if nsub == 1 else jnp.concatenate(atts[i], axis=1),
                                  vb_s[sls[i], :], preferred_element_type=F32) for i in n]
        return cs, accs

    tiles = range(SB_BQ // SB_BK)
    qs = [q[h * SB_BK:(h + 1) * SB_BK, :] for h in tiles]
    cs = [jnp.zeros((SB_BK, 1), F32) for _ in tiles]
    accs = [jnp.zeros((SB_BK, SB_HEAD), F32) for _ in tiles]
    base = qi * SB_BQ
    cs, accs = blocks(qs, [base + h * SB_BK for h in tiles], 1, cs, accs, True)
    c1, a1 = blocks(qs[1:], [base], 1, cs[1:], accs[1:], False)
    cs, accs = [cs[0]] + c1, [accs[0]] + a1

    def body(n, carry):
        k0 = base - (n + 1) * 2 * SB_BK
        c, a = blocks(qs, [k0, k0], 2, list(carry[0]), list(carry[1]), False)
        return tuple(c), tuple(a)

    cs, accs = lax.fori_loop(0, qi, body, (tuple(cs), tuple(accs)))
    for h in tiles:
        o_ref[h * SB_BK:(h + 1) * SB_BK, :] = accs[h].astype(o_ref.dtype)


def _stick_breaking(p, qg, kg, bsz, seq):
    assert SB_BQ == 2 * SB_BK
    nq = seq // SB_BQ
    qb, kb, vb = COL_SBQ // SB_HEAD, COL_SBK // SB_HEAD, COL_SBV // SB_HEAD
    return pl.pallas_call(
        functools.partial(_sb_kernel, seq=seq),
        grid=(bsz, SB_HEADS, nq),
        in_specs=[
            pl.BlockSpec((SB_BQ, SB_HEAD), lambda b, h, q: (b * nq + q, qb + h)),
            pl.BlockSpec((seq, SB_HEAD), lambda b, h, q: (b, kb + h)),
            pl.BlockSpec((seq, SB_HEAD), lambda b, h, q: (b, vb + h)),
            pl.BlockSpec((1, SB_HEAD), lambda b, h, q: (0, 0)),
            pl.BlockSpec((1, SB_HEAD), lambda b, h, q: (0, 0)),
        ],
        out_specs=pl.BlockSpec((SB_BQ, SB_HEAD), lambda b, h, q: (b * nq + q, h)),
        out_shape=jax.ShapeDtypeStruct((bsz * seq, SB_W), BF16),
        scratch_shapes=[pltpu.VMEM((seq, SB_HEAD), BF16), pltpu.VMEM((seq, SB_HEAD), BF16)],
        compiler_params=_cparams(("parallel", "parallel", "arbitrary")),
        name="stick_breaking",
    )(p, p, p, qg.reshape(1, -1), kg.reshape(1, -1))


PREP_HALO = 8


def _head_ones():
    r = lax.broadcasted_iota(jnp.int32, (PAIR, PAIR), 0) // RWKV_HEAD
    c = lax.broadcasted_iota(jnp.int32, (PAIR, PAIR), 1) // RWKV_HEAD
    return (r == c).astype(BF16)


def _rwkv_prep_kernel(p_ref, lora_ref, hp_ref, hlora_ref, mu_ref, mul_ref, w0_ref, wup_ref, a0_ref,
                      aup_ref, gup_ref, kk_ref, ka_ref, rk_ref,
                      r_o, k_o, v_o, lw_o, a_o, b_o, bonus_o, g_o, *, tm, seq):
    i = pl.program_id(0)
    first = (i * tm) % seq == 0

    def shifted(x_ref, h_ref, m_ref, c0, c1):
        x = x_ref[:, c0:c1]
        prev = pltpu.roll(x, 1, 0)
        row = lax.broadcasted_iota(jnp.int32, x.shape, 0)
        last = h_ref[PREP_HALO - 1:PREP_HALO, c0:c1]
        last = jnp.where(first, jnp.zeros_like(last), last)
        prev = jnp.where(row == 0, last, prev)
        return x + m_ref[:, c0:c1] * (prev - x)

    lo = shifted(lora_ref, hlora_ref, mul_ref, 0, LORA_COLS)
    w_lo = lo[:, 0:LORA_PAD]
    a_lo = lo[:, LORA_PAD:2 * LORA_PAD]
    g_lo = lo[:, 2 * LORA_PAD:]
    ones = _head_ones()

    for c0 in range(0, RWKV_W, 512):
        c1 = c0 + 512
        r = shifted(p_ref, hp_ref, mu_ref, c0, c1)
        k = shifted(p_ref, hp_ref, mu_ref, RWKV_W + c0, RWKV_W + c1)
        v = shifted(p_ref, hp_ref, mu_ref, 2 * RWKV_W + c0, 2 * RWKV_W + c1)
        y = w0_ref[:, c0:c1] + _dot(jnp.tanh(w_lo), wup_ref[:, c0:c1])
        log_w = -_softplus(-y) - 0.5
        lw_o[:, c0:c1] = -jnp.exp(log_w)
        iclr = jax.nn.sigmoid(a0_ref[:, c0:c1] + _dot(a_lo, aup_ref[:, c0:c1]))
        g_o[:, c0:c1] = _dot(jax.nn.sigmoid(g_lo), gup_ref[:, c0:c1])
        kk = k * kk_ref[:, c0:c1]
        kt = k * (1.0 + (iclr - 1.0) * ka_ref[:, c0:c1])
        rkr = r * kt * rk_ref[:, c0:c1]
        for t0 in range(0, 512, PAIR):
            t1 = t0 + PAIR
            kk_t = kk[:, t0:t1]
            ss = _dot_x3(kk_t * kk_t, ones)
            kk_n = kk_t * lax.rsqrt(jnp.maximum(ss, 1e-24))
            a_o[:, c0 + t0:c0 + t1] = -kk_n
            b_o[:, c0 + t0:c0 + t1] = kk_n * iclr[:, t0:t1]
            bonus_o[:, c0 + t0:c0 + t1] = _dot_x3(rkr[:, t0:t1], ones) * v[:, t0:t1]
        r_o[:, c0:c1] = r
        k_o[:, c0:c1] = kt
        v_o[:, c0:c1] = v


def _rwkv_prep(p, mu, mul, w0, wup, a0, aup, gup, kk, ka, rk, seq, tm=128):
    m = p.shape[0]
    lb = COL_LORA // LORA_COLS
    hmap = lambda i: jnp.maximum(i * (tm // PREP_HALO) - 1, 0)
    row = lambda a: a.reshape(1, -1)
    vec = pl.BlockSpec((1, RWKV_W), lambda i: (0, 0))
    out = jax.ShapeDtypeStruct((m, RWKV_W), F32)
    ospec = pl.BlockSpec((tm, RWKV_W), lambda i: (i, 0))
    return pl.pallas_call(
        functools.partial(_rwkv_prep_kernel, tm=tm, seq=seq),
        grid=(m // tm,),
        in_specs=[
            pl.BlockSpec((tm, 3 * RWKV_W), lambda i: (i, 0)),
            pl.BlockSpec((tm, LORA_COLS), lambda i: (i, lb)),
            pl.BlockSpec((PREP_HALO, 3 * RWKV_W), lambda i: (hmap(i), 0)),
            pl.BlockSpec((PREP_HALO, LORA_COLS), lambda i: (hmap(i), lb)),
            pl.BlockSpec((1, 3 * RWKV_W), lambda i: (0, 0)),
            pl.BlockSpec((1, LORA_COLS), lambda i: (0, 0)),
            vec,
            pl.BlockSpec((LORA_PAD, RWKV_W), lambda i: (0, 0)),
            vec,
            pl.BlockSpec((LORA_PAD, RWKV_W), lambda i: (0, 0)),
            pl.BlockSpec((GATE_LORA, RWKV_W), lambda i: (0, 0)),
            vec, vec, vec,
        ],
        out_specs=[ospec] * 8,
        out_shape=[out] * 8,
        compiler_params=_cparams(("parallel",)),
        name="rwkv_prep",
    )(p, p, p, p, row(mu), row(mul), row(w0), wup, row(a0), aup, gup, row(kk), row(ka), row(rk))


SCAN_PAIRS = 2
GN_ROWS = 256


def _rwkv_scan_kernel(r_ref, k_ref, v_ref, lw_ref, a_ref, b_ref, bonus_ref, g_ref, gng_ref, gnb_ref,
                      o_ref, s_ref, rq_s, oin_s, tr_s, dl_s, *, tblk):
    C = RWKV_C
    t_id = pl.program_id(2)

    @pl.when(t_id == 0)
    def _():
        s_ref[...] = jnp.zeros_like(s_ref)

    lane = lax.broadcasted_iota(jnp.int32, (1, PAIR), 1)
    m0 = (lane < RWKV_HEAD).astype(F32)
    m1 = 1.0 - m0
    ti = lax.broadcasted_iota(jnp.int32, (C, C), 0)
    ii = lax.broadcasted_iota(jnp.int32, (C, C), 1)
    tri_incl = (ii <= ti).astype(BF16)
    t2 = lax.broadcasted_iota(jnp.int32, (C, 2 * C), 0)
    i2 = lax.broadcasted_iota(jnp.int32, (C, 2 * C), 1) % C
    strict2 = (i2 < t2).astype(F32)
    incl2 = (i2 <= t2).astype(F32)
    rr = lax.broadcasted_iota(jnp.int32, (PAIR, PAIR), 0)
    cc = lax.broadcasted_iota(jnp.int32, (PAIR, PAIR), 1)
    bdmask = (rr // RWKV_HEAD == cc // RWKV_HEAD).astype(F32)
    eye = (rr == cc).astype(F32)
    ones = _head_ones()

    def stack_heads(x):
        return jnp.concatenate([x * m0, x * m1], axis=0)

    def chunk_pair(c, pi):
        sl = pl.ds(pl.multiple_of(c * C, C), C)
        ln = slice(pi * PAIR, (pi + 1) * PAIR)
        lw = lw_ref[sl, ln]
        r, k, v, a, b = r_ref[sl, ln], k_ref[sl, ln], v_ref[sl, ln], a_ref[sl, ln], b_ref[sl, ln]
        cum = _x3_dot(tri_incl, lw)
        tot = cum[C - 1:C, :]
        e_neg = jnp.exp(-cum)
        e_end = jnp.exp(tot - cum)
        at = a * jnp.exp(cum - lw)
        rt = r * jnp.exp(cum)
        bh, kh = b * e_neg, k * e_neg
        bend, kend = b * e_end, k * e_end
        gram = _dot_nt(jnp.concatenate([at, rt], axis=0),
                       jnp.concatenate([stack_heads(bh), stack_heads(kh)], axis=0))
        m_ab = gram[0:C, 0:2 * C] * strict2
        m_ak = gram[0:C, 2 * C:] * strict2
        q_b = gram[C:, 0:2 * C] * incl2
        q_k = gram[C:, 2 * C:] * incl2
        vv = stack_heads(v)
        u_rhs = _dot(m_ak, vv)
        bd = stack_heads(m_ab)
        t_inv = eye + bd
        pw = bd
        for _ in range(5):
            pw = _dot(pw, pw)
            t_inv = t_inv + _dot(t_inv, pw)
        t_ls = t_inv[0:C, :] + t_inv[C:, :]
        sol = _dot(t_ls, jnp.concatenate(
            [jnp.concatenate([at * m0, u_rhs * m0], axis=1),
             jnp.concatenate([at * m1, u_rhs * m1], axis=1)], axis=0))
        wa, u0 = sol[:, 0:PAIR], sol[:, PAIR:]
        y = _dot(q_b, jnp.concatenate(
            [jnp.concatenate([wa * m0, u0 * m0], axis=1),
             jnp.concatenate([wa * m1, u0 * m1], axis=1)], axis=0))
        rq_s[sl, ln] = rt + y[:, 0:PAIR]
        oin_s[sl, ln] = y[:, PAIR:] + _dot(q_k, vv)
        tr_s[pi, c] = _dot_tn(wa, bend) * bdmask + eye * jnp.exp(tot)
        dl_s[pi, c] = (_dot_tn(u0, bend) + _dot_tn(v, kend)) * bdmask

    def chunk(c, carry):
        for pi in range(SCAN_PAIRS):
            chunk_pair(c, pi)
        return carry

    lax.fori_loop(0, tblk // C, chunk, 0)

    def scan(c, carry):
        sl = pl.ds(pl.multiple_of(c * C, C), C)
        for pi in range(SCAN_PAIRS):
            ln = slice(pi * PAIR, (pi + 1) * PAIR)
            s = s_ref[pi]
            oin_s[sl, ln] = _dot_nt(rq_s[sl, ln], s) + oin_s[sl, ln]
            s_ref[pi] = _dot(s, tr_s[pi, c]) + dl_s[pi, c]
        return carry

    lax.fori_loop(0, tblk // C, scan, 0)

    for pi in range(SCAN_PAIRS):
        ln = slice(pi * PAIR, (pi + 1) * PAIR)
        for r0 in range(0, tblk, GN_ROWS):
            rows = slice(r0, r0 + GN_ROWS)
            o = oin_s[rows, ln]
            mean = _dot_x3(o, ones) * (1.0 / RWKV_HEAD)
            d = o - mean
            var = _dot_x3(d * d, ones) * (1.0 / RWKV_HEAD)
            on = d * lax.rsqrt(var + GN_EPS) * gng_ref[:, ln] + gnb_ref[:, ln]
            o_ref[rows, ln] = ((on + bonus_ref[rows, ln]) * g_ref[rows, ln]).astype(o_ref.dtype)


def _rwkv_scan(r, k, v, lw, a, b, bonus, g, gng, gnb, bsz, seq, tblk=512):
    nt = seq // tblk
    width = SCAN_PAIRS * PAIR
    tile = pl.BlockSpec((tblk, width), lambda bi, h, t: (bi * nt + t, h))
    vec = pl.BlockSpec((1, width), lambda bi, h, t: (0, h))
    nch = tblk // RWKV_C
    return pl.pallas_call(
        functools.partial(_rwkv_scan_kernel, tblk=tblk),
        grid=(bsz, RWKV_W // width, nt),
        in_specs=[tile] * 8 + [vec, vec],
        out_specs=tile,
        out_shape=jax.ShapeDtypeStruct((bsz * seq, RWKV_W), BF16),
        scratch_shapes=[pltpu.VMEM((SCAN_PAIRS, PAIR, PAIR), F32),
                        pltpu.VMEM((tblk, width), F32), pltpu.VMEM((tblk, width), F32),
                        pltpu.VMEM((SCAN_PAIRS, nch, PAIR, PAIR), F32),
                        pltpu.VMEM((SCAN_PAIRS, nch, PAIR, PAIR), F32)],
        compiler_params=_cparams(("parallel", "parallel", "arbitrary")),
        name="rwkv_scan",
    )(r, k, v, lw, a, b, bonus, g, gng.reshape(1, -1), gnb.reshape(1, -1))


QUAD_HEADS = 2
QUAD = QUAD_HEADS * RWKV_HEAD
SCAN_GROUPS = 16
SCAN_ROWS = 256


def _rwkv_quad_kernel(r_ref, k_ref, v_ref, lw_ref, a_ref, b_ref, bonus_ref, g_ref, gng_ref, gnb_ref,
                      o_ref, s_ref, rq_s, oin_s, tr_s, dl_s, *, tblk):
    C = RWKV_C
    HC = QUAD_HEADS * C
    G = range(SCAN_GROUPS)
    t_id = pl.program_id(2)

    @pl.when(t_id == 0)
    def _():
        s_ref[...] = jnp.zeros_like(s_ref)

    lane = lax.broadcasted_iota(jnp.int32, (1, QUAD), 1) // RWKV_HEAD
    hm = [(lane == h).astype(F32) for h in range(QUAD_HEADS)]
    ti = lax.broadcasted_iota(jnp.int32, (C, C), 0)
    ii = lax.broadcasted_iota(jnp.int32, (C, C), 1)
    tri_incl = (ii <= ti).astype(BF16)
    t4 = lax.broadcasted_iota(jnp.int32, (C, HC), 0)
    i4 = lax.broadcasted_iota(jnp.int32, (C, HC), 1) % C
    strict4 = (i4 < t4).astype(F32)
    incl4 = (i4 <= t4).astype(F32)
    rr = lax.broadcasted_iota(jnp.int32, (QUAD, QUAD), 0)
    cc = lax.broadcasted_iota(jnp.int32, (QUAD, QUAD), 1)
    bdmask = (rr // RWKV_HEAD == cc // RWKV_HEAD).astype(F32)
    eye = (rr == cc).astype(F32)
    ones = (rr // RWKV_HEAD == cc // RWKV_HEAD).astype(BF16)

    def stack_heads(x):
        return jnp.concatenate([x * m for m in hm], axis=0)

    def stack_heads2(x, y):
        return jnp.concatenate([jnp.concatenate([x * m, y * m], axis=1) for m in hm], axis=0)

    def chunk(c, carry):
        sl = pl.ds(pl.multiple_of(c * C, C), C)
        lns = [slice(g * QUAD, (g + 1) * QUAD) for g in G]
        lw = [lw_ref[sl, ln] for ln in lns]
        cum = [_x3_dot(tri_incl, x) for x in lw]
        tot = [x[C - 1:C, :] for x in cum]
        e_neg = [jnp.exp(-x) for x in cum]
        e_end = [jnp.exp(t - x) for t, x in zip(tot, cum)]
        at = [a_ref[sl, ln] * jnp.exp(x - w) for ln, x, w in zip(lns, cum, lw)]
        rt = [r_ref[sl, ln] * jnp.exp(x) for ln, x in zip(lns, cum)]
        b = [b_ref[sl, ln] for ln in lns]
        k = [k_ref[sl, ln] for ln in lns]
        v = [v_ref[sl, ln] for ln in lns]
        gram = [_dot_nt(jnp.concatenate([at[g], rt[g]], axis=0),
                        jnp.concatenate([stack_heads(b[g] * e_neg[g]), stack_heads(k[g] * e_neg[g])], axis=0))
                for g in G]
        m_ak = [x[0:C, HC:] * strict4 for x in gram]
        q_b = [x[C:, 0:HC] * incl4 for x in gram]
        q_k = [x[C:, HC:] * incl4 for x in gram]
        vv = [stack_heads(x) for x in v]
        u_rhs = [_dot(m_ak[g], vv[g]) for g in G]
        pw = [stack_heads(x[0:C, 0:HC] * strict4) for x in gram]
        t_inv = [eye + x for x in pw]
        pw = [_dot(x, x) for x in pw]
        for _ in range(4):
            both = [_dot(jnp.concatenate([x, t], axis=0), x) for t, x in zip(t_inv, pw)]
            pw = [x[0:HC, :] for x in both]
            t_inv = [t + x[HC:, :] for t, x in zip(t_inv, both)]
        t_inv = [t + _dot(t, x) for t, x in zip(t_inv, pw)]
        t_ls = [sum(t[h * C:(h + 1) * C, :] for h in range(1, QUAD_HEADS)) + t[0:C, :] for t in t_inv]
        sol = [_dot(t_ls[g], stack_heads2(at[g], u_rhs[g])) for g in G]
        wa = [x[:, 0:QUAD] for x in sol]
        u0 = [x[:, QUAD:] for x in sol]
        y = [_dot(q_b[g], stack_heads2(wa[g], u0[g])) for g in G]
        o_in2 = [_dot(q_k[g], vv[g]) for g in G]
        bend = [b[g] * e_end[g] for g in G]
        tr = [_dot_tn(wa[g], bend[g]) for g in G]
        dl = [_dot_tn(jnp.concatenate([u0[g], v[g]], axis=0),
                      jnp.concatenate([bend[g], k[g] * e_end[g]], axis=0)) for g in G]
        for g in G:
            rq_s[sl, lns[g]] = rt[g] + y[g][:, 0:QUAD]
            oin_s[sl, lns[g]] = y[g][:, QUAD:] + o_in2[g]
            tr_s[g, c] = tr[g] * bdmask + eye * jnp.exp(tot[g])
            dl_s[g, c] = dl[g] * bdmask
        return carry

    lax.fori_loop(0, tblk // C, chunk, 0)

    def scan(c, carry):
        sl = pl.ds(pl.multiple_of(c * C, C), C)
        for g in G:
            ln = slice(g * QUAD, (g + 1) * QUAD)
            s = s_ref[g]
            oin_s[sl, ln] = _dot_nt(rq_s[sl, ln], s) + oin_s[sl, ln]
            s_ref[g] = _dot(s, tr_s[g, c]) + dl_s[g, c]
        return carry

    lax.fori_loop(0, tblk // C, scan, 0)

    for g in G:
        ln = slice(g * QUAD, (g + 1) * QUAD)
        for r0 in range(0, tblk, GN_ROWS):
            rows = slice(r0, r0 + GN_ROWS)
            o = oin_s[rows, ln]
            mean = _dot_x3(o, ones) * (1.0 / RWKV_HEAD)
            d = o - mean
            var = _dot_x2(d * d, ones) * (1.0 / RWKV_HEAD)
            on = d * lax.rsqrt(var + GN_EPS) * gng_ref[:, ln] + gnb_ref[:, ln]
            o_ref[rows, ln] = ((on + bonus_ref[rows, ln]) * g_ref[rows, ln]).astype(o_ref.dtype)


def _rwkv_quad_scan(r, k, v, lw, a, b, bonus, g, gng, gnb, bsz, seq, tblk=512):
    nt = seq // tblk
    width = SCAN_GROUPS * QUAD
    tile = pl.BlockSpec((tblk, width), lambda bi, h, t: (bi * nt + t, h))
    vec = pl.BlockSpec((1, width), lambda bi, h, t: (0, h))
    nch = tblk // RWKV_C
    return pl.pallas_call(
        functools.partial(_rwkv_quad_kernel, tblk=tblk),
        grid=(bsz, RWKV_W // width, nt),
        in_specs=[tile] * 8 + [vec, vec],
        out_specs=tile,
        out_shape=jax.ShapeDtypeStruct((bsz * seq, RWKV_W), BF16),
        scratch_shapes=[pltpu.VMEM((SCAN_GROUPS, QUAD, QUAD), F32),
                        pltpu.VMEM((tblk, width), F32), pltpu.VMEM((tblk, width), F32),
                        pltpu.VMEM((SCAN_GROUPS, nch, QUAD, QUAD), F32),
                        pltpu.VMEM((SCAN_GROUPS, nch, QUAD, QUAD), F32)],
        compiler_params=_cparams(("parallel", "parallel", "arbitrary")),
        name="rwkv_scan",
    )(r, k, v, lw, a, b, bonus, g, gng.reshape(1, -1), gnb.reshape(1, -1))


def _relayout_cols(w, axis):
    def sl(a, b):
        idx = [slice(None)] * w.ndim
        idx[axis] = slice(a, b)
        return w[tuple(idx)]
    pad_shape = list(w.shape)
    pad_shape[axis] = LORA_PAD - DECAY_LORA
    z = jnp.zeros(pad_shape, w.dtype)
    lo = 3 * RWKV_W
    rest = lo + DECAY_LORA + ICLR_LORA + GATE_LORA
    return jnp.concatenate(
        [sl(0, lo), sl(rest, w.shape[axis]), sl(lo, lo + DECAY_LORA), z,
         sl(lo + DECAY_LORA, lo + DECAY_LORA + ICLR_LORA), z,
         sl(lo + DECAY_LORA + ICLR_LORA, rest)], axis=axis)


def _pad_rows(w):
    return jnp.concatenate([w, jnp.zeros((LORA_PAD - w.shape[0], w.shape[1]), w.dtype)], axis=0)


def _layer(x, bsz, seq, layer, w_in, w_out, attn_norm, ffn_norm, mu, w0, w_up, a0, a_up, g_up, k_k, k_a, r_k,
           gn_g, gn_b, conv_w, conv_b, ln_g, ln_b, q_norm, k_norm, ffn_up, ffn_conv, ffn_down, tm_mm=1024):
    h = _rmsnorm(x, attn_norm)
    p = _matmul([h], w_in, layer, tm=tm_mm, name="proj_in")
    mu_p, mu_l = mu[:3 * RWKV_W], mu[COL_LORA:]
    r, kt, v, lw, a, b, bonus, g = _rwkv_prep(p, mu_p, mu_l, w0, w_up, a0, a_up, g_up, k_k, k_a,
                                              r_k.reshape(-1), seq)
    y_a = _rwkv_quad_scan(r, kt, v, lw, a, b, bonus, g, gn_g, gn_b, bsz, seq, tblk=min(SCAN_ROWS, seq))
    y_b = _conformer(p, conv_w, conv_b, ln_g, ln_b, seq)
    y_c = _stick_breaking(p, q_norm, k_norm, bsz, seq)
    x = _matmul([y_a, y_b, y_c], w_out, layer, residual=x, tm=tm_mm, name="proj_out")
    h = _rmsnorm(x, ffn_norm)
    act = _ffn_up(h, ffn_up, layer, ffn_conv, seq, tm=tm_mm)
    return _matmul([act], ffn_down, layer, residual=x, tm=min(512, tm_mm), name="ffn_down")


def kernel(x, w_in, w_out, attn_norm, ffn_norm, rwkv_mu, rwkv_w0, rwkv_w_up, rwkv_a0, rwkv_a_up,
           rwkv_g_up, rwkv_k_k, rwkv_k_a, rwkv_r_k, rwkv_gn_g, rwkv_gn_b, conv_w, conv_b, conv_ln_g,
           conv_ln_b, sb_q_norm, sb_k_norm, ffn_up, ffn_conv, ffn_down):
    bsz, seq, d = x.shape
    x = x.reshape(bsz * seq, d)
    w_in_b = _relayout_cols(w_in.astype(BF16), 2)
    w_out_b, ffn_up_b, ffn_down_b = w_out.astype(BF16), ffn_up.astype(BF16), ffn_down.astype(BF16)
    for l in range(w_in.shape[0]):
        mu_full = jnp.concatenate([rwkv_mu[l], jnp.zeros((w_in.shape[2] - rwkv_mu.shape[1],), F32)])
        mu_re = _relayout_cols(mu_full, 0)
        x = _layer(
            x, bsz, seq, l, w_in_b, w_out_b, attn_norm[l], ffn_norm[l],
            mu_re, rwkv_w0[l], _pad_rows(rwkv_w_up[l]).astype(BF16), rwkv_a0[l],
            _pad_rows(rwkv_a_up[l]).astype(BF16), rwkv_g_up[l].astype(BF16), rwkv_k_k[l], rwkv_k_a[l],
            rwkv_r_k[l], rwkv_gn_g[l], rwkv_gn_b[l], conv_w[l], conv_b[l], conv_ln_g[l], conv_ln_b[l],
            sb_q_norm[l], sb_k_norm[l], ffn_up_b, ffn_conv[l], ffn_down_b,
            tm_mm=min(1024, seq))
    return x.reshape(bsz, seq, d)
```

```python
import functools

import jax
import jax.numpy as jnp
from jax import lax
from jax.experimental import pallas as pl
from jax.experimental.pallas import tpu as pltpu

F32 = jnp.float32
BF16 = jnp.bfloat16

D_MODEL = 4096
DEPTH = 4
RWKV_HEAD = 64
RWKV_W = 2048
RWKV_HEADS = RWKV_W // RWKV_HEAD
DECAY_LORA = 96
ICLR_LORA = 96
GATE_LORA = 256
CONV_W = 1024
CONV_K = 31
SB_HEAD = 128
SB_W = 1024
SB_HEADS = SB_W // SB_HEAD
D_FF = 2 * D_MODEL
FFN_CONV_K = 3
NORM_EPS = 1e-6
LN_EPS = 1e-5
GN_EPS = RWKV_HEAD * 1e-5

LANES = 128
SUBLANES = 8
LORA_PAD = 128
COL_RKV = 0
COL_CVAL = 3 * RWKV_W
COL_CGATE = COL_CVAL + CONV_W
COL_SBQ = COL_CGATE + CONV_W
COL_SBK = COL_SBQ + SB_W
COL_SBV = COL_SBK + SB_W
COL_LORA = COL_SBV + SB_W
LORA_COLS = 2 * LORA_PAD + GATE_LORA
PROJ_PAD = COL_LORA + LORA_COLS
VMEM_LIMIT = 56 * 1024 * 1024

RWKV_C = 64
PAIR = 2 * RWKV_HEAD


def _cparams(sem):
    return pltpu.CompilerParams(dimension_semantics=sem, vmem_limit_bytes=VMEM_LIMIT)


def _dot(a, b):
    return jnp.dot(a.astype(BF16), b.astype(BF16), preferred_element_type=F32)


def _dot_nt(a, b):
    return lax.dot_general(a.astype(BF16), b.astype(BF16), (((1,), (1,)), ((), ())),
                           preferred_element_type=F32)


def _dot_tn(a, b):
    return lax.dot_general(a.astype(BF16), b.astype(BF16), (((0,), (0,)), ((), ())),
                           preferred_element_type=F32)


def _split3(x):
    hi = x.astype(BF16)
    r1 = x - hi.astype(F32)
    mid = r1.astype(BF16)
    lo = (r1 - mid.astype(F32)).astype(BF16)
    return hi, mid, lo


def _dot_x3(a, b_exact):
    hi, mid, lo = _split3(a)
    b = b_exact.astype(BF16)
    return (jnp.dot(hi, b, preferred_element_type=F32)
            + jnp.dot(mid, b, preferred_element_type=F32)
            + jnp.dot(lo, b, preferred_element_type=F32))


def _x3_dot(a_exact, b):
    hi, mid, lo = _split3(b)
    a = a_exact.astype(BF16)
    return (jnp.dot(a, hi, preferred_element_type=F32)
            + jnp.dot(a, mid, preferred_element_type=F32)
            + jnp.dot(a, lo, preferred_element_type=F32))


def _dot_x2(a, b_exact):
    hi = a.astype(BF16)
    lo = (a - hi.astype(F32)).astype(BF16)
    b = b_exact.astype(BF16)
    return jnp.dot(hi, b, preferred_element_type=F32) + jnp.dot(lo, b, preferred_element_type=F32)


_SB_SUFFIX_DOT = _dot


def _dot_hi(a, b):
    ah = a.astype(BF16)
    al = (a - ah.astype(F32)).astype(BF16)
    bh = b.astype(BF16)
    bl = (b - bh.astype(F32)).astype(BF16)
    return (jnp.dot(ah, bh, preferred_element_type=F32)
            + jnp.dot(al, bh, preferred_element_type=F32)
            + jnp.dot(ah, bl, preferred_element_type=F32))


def _neg_abs(x):
    bits = lax.bitcast_convert_type(x, jnp.int32) | jnp.int32(-2 ** 31)
    return lax.bitcast_convert_type(bits, F32)


def _softplus(x):
    return jnp.maximum(x, 0.0) + jnp.log(1.0 + jnp.exp(_neg_abs(x)))


def _rmsnorm_kernel(x_ref, g_ref, o_ref):
    x = x_ref[...]
    ms = jnp.mean(x * x, axis=-1, keepdims=True)
    o_ref[...] = (x * lax.rsqrt(ms + NORM_EPS) * g_ref[...]).astype(o_ref.dtype)


def _rmsnorm(x, g, tm=256):
    m, d = x.shape
    return pl.pallas_call(
        _rmsnorm_kernel,
        grid=(m // tm,),
        in_specs=[pl.BlockSpec((tm, d), lambda i: (i, 0)),
                  pl.BlockSpec((1, d), lambda i: (0, 0))],
        out_specs=pl.BlockSpec((tm, d), lambda i: (i, 0)),
        out_shape=jax.ShapeDtypeStruct((m, d), BF16),
        compiler_params=_cparams(("parallel",)),
        name="rmsnorm",
    )(x, g.reshape(1, d))


def _mm_kernel(*refs, n_lhs, has_res):
    o_ref = refs[-1]
    acc = None
    for a_ref, w_ref in zip(refs[:n_lhs], refs[n_lhs:2 * n_lhs]):
        d = jnp.dot(a_ref[...], w_ref[...], preferred_element_type=F32)
        acc = d if acc is None else acc + d
    if has_res:
        acc = acc + refs[2 * n_lhs][...]
    o_ref[...] = acc.astype(o_ref.dtype)


def _matmul(lhs_list, w, layer, residual=None, tm=1024, tn=512, name="matmul"):
    m = lhs_list[0].shape[0]
    n = w.shape[2]
    in_specs, args = [], []
    for a in lhs_list:
        in_specs.append(pl.BlockSpec((tm, a.shape[1]), lambda i, j: (i, 0)))
        args.append(a)
    row = 0
    for a in lhs_list:
        k = a.shape[1]
        assert row % k == 0
        in_specs.append(pl.BlockSpec((pl.Squeezed(), k, tn),
                                     functools.partial(lambda i, j, rb: (layer, rb, j), rb=row // k)))
        args.append(w)
        row += k
    assert row == w.shape[1]
    if residual is not None:
        in_specs.append(pl.BlockSpec((tm, tn), lambda i, j: (i, j)))
        args.append(residual)
    return pl.pallas_call(
        functools.partial(_mm_kernel, n_lhs=len(lhs_list), has_res=residual is not None),
        grid=(m // tm, n // tn),
        in_specs=in_specs,
        out_specs=pl.BlockSpec((tm, tn), lambda i, j: (i, j)),
        out_shape=jax.ShapeDtypeStruct((m, n), F32),
        compiler_params=_cparams(("parallel", "arbitrary")),
        name=name,
    )(*args)


FFN_CARRY = 8
FFN_SUB = 256


def _ffn_up_kernel(h_ref, wg_ref, wv_ref, cg_ref, cv_ref, o_ref, *scratch, tm, tn, seq):
    i = pl.program_id(1)
    first = (i * tm) % seq == 0
    nsub = tn // FFN_SUB
    sg, sv = scratch[:nsub], scratch[nsub:]

    @pl.when(first)
    def _():
        for s in scratch:
            s[0:FFN_CARRY, :] = jnp.zeros((FFN_CARRY, FFN_SUB), F32)

    @pl.when(jnp.logical_not(first))
    def _():
        for s in scratch:
            s[0:FFN_CARRY, :] = s[tm:tm + FFN_CARRY, :]

    def up(w_ref, s, n):
        s[FFN_CARRY:FFN_CARRY + tm, :] = jnp.dot(h_ref[...], w_ref[:, n * FFN_SUB:(n + 1) * FFN_SUB],
                                                 preferred_element_type=F32)

    def conv(c_ref, s, n):
        acc = None
        for k in range(FFN_CONV_K):
            off = FFN_CARRY - (FFN_CONV_K - 1) + k
            term = c_ref[k:k + 1, n * FFN_SUB:(n + 1) * FFN_SUB] * s[off:off + tm, :]
            acc = term if acc is None else acc + term
        return acc

    def gate(n):
        g = conv(cg_ref, sg[n], n)
        v = conv(cv_ref, sv[n], n)
        o_ref[:, n * FFN_SUB:(n + 1) * FFN_SUB] = (g * jax.nn.sigmoid(g) * v).astype(o_ref.dtype)

    up(wg_ref, sg[0], 0)
    up(wv_ref, sv[0], 0)
    for n in range(1, nsub):
        up(wg_ref, sg[n], n)
        gate(n - 1)
        up(wv_ref, sv[n], n)
    gate(nsub - 1)


def _ffn_up(h, w_up, layer, conv, seq, tm=1024, tn=512):
    m, d = h.shape
    nj = D_FF // tn
    return pl.pallas_call(
        functools.partial(_ffn_up_kernel, tm=tm, tn=tn, seq=seq),
        grid=(nj, m // tm),
        in_specs=[
            pl.BlockSpec((tm, d), lambda j, i: (i, 0)),
            pl.BlockSpec((pl.Squeezed(), d, tn), lambda j, i: (layer, 0, j)),
            pl.BlockSpec((pl.Squeezed(), d, tn), lambda j, i: (layer, 0, j + nj)),
            pl.BlockSpec((FFN_CONV_K, tn), lambda j, i: (0, j)),
            pl.BlockSpec((FFN_CONV_K, tn), lambda j, i: (0, j + nj)),
        ],
        out_specs=pl.BlockSpec((tm, tn), lambda j, i: (i, j)),
        out_shape=jax.ShapeDtypeStruct((m, D_FF), BF16),
        scratch_shapes=[pltpu.VMEM((FFN_CARRY + tm, FFN_SUB), F32)] * (2 * (tn // FFN_SUB)),
        compiler_params=_cparams(("parallel", "arbitrary")),
        name="ffn_up_conv_gate",
    )(h, w_up, w_up, conv, conv)


CONF_HALO = 32
CONF_ROWS = 32
CONF_COLS = 256


def _conformer_kernel(val_ref, gate_ref, hval_ref, hgate_ref, cw_ref, cb_ref, lng_ref, lnb_ref,
                      o_ref, glu_s, u_s, *, tm, seq):
    i = pl.program_id(0)
    first = (i * tm) % seq == 0
    hglu = hval_ref[...] * jax.nn.sigmoid(hgate_ref[...])
    glu_s[0, 0:CONF_HALO, :] = jnp.where(first, jnp.zeros_like(hglu), hglu)
    glu_s[0, CONF_HALO:CONF_HALO + tm, :] = val_ref[...] * jax.nn.sigmoid(gate_ref[...])
    for b in range(1, SUBLANES):
        glu_s[b, SUBLANES:CONF_HALO + tm, :] = glu_s[0, SUBLANES - b:CONF_HALO + tm - b, :]
    for r0 in range(0, tm, CONF_ROWS):
        for c0 in range(0, CONV_W, CONF_COLS):
            acc = jnp.zeros((CONF_ROWS, CONF_COLS), F32) + cb_ref[:, c0:c0 + CONF_COLS]
            for k in range(CONV_K):
                a, b = divmod(CONV_K - 1 - k, SUBLANES)
                start = CONF_HALO + r0 - SUBLANES * a
                acc = acc + cw_ref[k:k + 1, c0:c0 + CONF_COLS] * glu_s[b, start:start + CONF_ROWS,
                                                                   c0:c0 + CONF_COLS]
            u_s[r0:r0 + CONF_ROWS, c0:c0 + CONF_COLS] = acc
    u = u_s[...]
    mean = jnp.mean(u, axis=-1, keepdims=True)
    d = u - mean
    var = jnp.mean(d * d, axis=-1, keepdims=True)
    y = d * lax.rsqrt(var + LN_EPS) * lng_ref[...] + lnb_ref[...]
    o_ref[...] = (y * jax.nn.sigmoid(y)).astype(o_ref.dtype)


def _conformer(p, cw, cb, lng, lnb, seq, tm=256):
    m = p.shape[0]
    vb, gb = COL_CVAL // CONV_W, COL_CGATE // CONV_W
    hmap = lambda i: jnp.maximum(i * (tm // CONF_HALO) - 1, 0)
    return pl.pallas_call(
        functools.partial(_conformer_kernel, tm=tm, seq=seq),
        grid=(m // tm,),
        in_specs=[
            pl.BlockSpec((tm, CONV_W), lambda i: (i, vb)),
            pl.BlockSpec((tm, CONV_W), lambda i: (i, gb)),
            pl.BlockSpec((CONF_HALO, CONV_W), lambda i: (hmap(i), vb)),
            pl.BlockSpec((CONF_HALO, CONV_W), lambda i: (hmap(i), gb)),
            pl.BlockSpec((CONV_K, CONV_W), lambda i: (0, 0)),
            pl.BlockSpec((1, CONV_W), lambda i: (0, 0)),
            pl.BlockSpec((1, CONV_W), lambda i: (0, 0)),
            pl.BlockSpec((1, CONV_W), lambda i: (0, 0)),
        ],
        out_specs=pl.BlockSpec((tm, CONV_W), lambda i: (i, 0)),
        out_shape=jax.ShapeDtypeStruct((m, CONV_W), BF16),
        scratch_shapes=[pltpu.VMEM((SUBLANES, CONF_HALO + tm, CONV_W), F32),
                        pltpu.VMEM((tm, CONV_W), F32)],
        compiler_params=_cparams(("parallel",)),
        name="conformer_conv",
    )(p, p, p, p, cw, cb.reshape(1, -1), lng.reshape(1, -1), lnb.reshape(1, -1))


SB_BQ = 512
SB_BK = 256
SB_PREP_ROWS = 512
SB_GROUP = 8
LOG2E = 1.4426950408889634
LN2 = 0.6931471805599453


def _sb_kernel(q_ref, k_ref, v_ref, qg_ref, kg_ref, o_ref, kn_s, vb_s, *, seq):
    qi = pl.program_id(2)

    @pl.when(qi == 0)
    def _():
        def body(c, carry):
            sl = pl.ds(pl.multiple_of(c * SB_PREP_ROWS, SB_PREP_ROWS), SB_PREP_ROWS)
            k = k_ref[sl, :]
            ms = jnp.mean(k * k, axis=-1, keepdims=True)
            kn_s[sl, :] = (k * lax.rsqrt(ms + NORM_EPS) * kg_ref[...]).astype(BF16)
            vb_s[sl, :] = v_ref[sl, :].astype(BF16)
            return carry
        lax.fori_loop(0, seq // SB_PREP_ROWS, body, 0)

    q = q_ref[...]
    ms = jnp.mean(q * q, axis=-1, keepdims=True)
    q = (q * lax.rsqrt(ms + NORM_EPS) * qg_ref[...] * (SB_HEAD ** -0.5 * LOG2E)).astype(BF16)

    row = lax.broadcasted_iota(jnp.int32, (SB_BK, SB_BK), 0)
    col = lax.broadcasted_iota(jnp.int32, (SB_BK, SB_BK), 1)
    later = (row > col).astype(BF16)
    causal = col < row

    def blocks(qs, k0s, nsub, cs, accs, masked):
        n = range(len(qs))
        sls = [pl.ds(pl.multiple_of(k0, SB_BK), nsub * SB_BK) for k0 in k0s]
        zs = [lax.dot_general(qs[i], kn_s[sls[i], :], (((1,), (1,)), ((), ())), preferred_element_type=F32)
              for i in n]
        sps = [jnp.maximum(z, 0.0) + jnp.log(1.0 + jnp.exp2(_neg_abs(z))) * (1.0 / LN2) for z in zs]
        if masked:
            sps = [jnp.where(causal, sp, 0.0) for sp in sps]
        atts = [[None] * nsub for _ in n]
        for j in reversed(range(nsub)):
            cols = slice(j * SB_BK, (j + 1) * SB_BK)
            sp_j = [sp[:, cols] for sp in sps]
            suffix = [_SB_SUFFIX_DOT(x, later) for x in sp_j]
            att = [jnp.exp2((zs[i][:, cols] - sp_j[i]) - (suffix[i] + cs[i])) for i in n]
            if masked:
                att = [jnp.where(causal, x, 0.0) for x in att]
            for i in n:
                atts[i][j] = att[i].astype(BF16)
            cs = [cs[i] + jnp.sum(sp_j[i], axis=-1, keepdims=True) for i in n]
        accs = [accs[i] + jnp.dot(atts[i][0] if nsub == 1 else jnp.concatenate(atts[i], axis=1),
                                  vb_s[sls[i], :], preferred_element_type=F32) for i in n]
        return cs, accs

    tiles = range(SB_BQ // SB_BK)
    qs = [q[h * SB_BK:(h + 1) * SB_BK, :] for h in tiles]
    cs = [jnp.zeros((SB_BK, 1), F32) for _ in tiles]
    accs = [jnp.zeros((SB_BK, SB_HEAD), F32) for _ in tiles]
    base = qi * SB_BQ
    cs, accs = blocks(qs, [base + h * SB_BK for h in tiles], 1, cs, accs, True)
    c1, a1 = blocks(qs[1:], [base], 1, cs[1:], accs[1:], False)
    cs, accs = [cs[0]] + c1, [accs[0]] + a1

    def run(nsub, start):
        def body(n, carry):
            k0 = start - (n + 1) * nsub * SB_BK
            c, a = blocks(qs, [k0, k0], nsub, list(carry[0]), list(carry[1]), False)
            return tuple(c), tuple(a)
        return body

    pairs_per_group = SB_GROUP // 2
    rem = qi % pairs_per_group
    carry = lax.fori_loop(0, rem, run(2, base), (tuple(cs), tuple(accs)))
    cs, accs = lax.fori_loop(0, qi // pairs_per_group, run(SB_GROUP, base - rem * 2 * SB_BK), carry)
    for h in tiles:
        o_ref[h * SB_BK:(h + 1) * SB_BK, :] = accs[h].astype(o_ref.dtype)


def _stick_breaking(p, qg, kg, bsz, seq):
    assert SB_BQ == 2 * SB_BK
    nq = seq // SB_BQ
    qb, kb, vb = COL_SBQ // SB_HEAD, COL_SBK // SB_HEAD, COL_SBV // SB_HEAD
    return pl.pallas_call(
        functools.partial(_sb_kernel, seq=seq),
        grid=(bsz, SB_HEADS, nq),
        in_specs=[
            pl.BlockSpec((SB_BQ, SB_HEAD), lambda b, h, q: (b * nq + q, qb + h)),
            pl.BlockSpec((seq, SB_HEAD), lambda b, h, q: (b, kb + h)),
            pl.BlockSpec((seq, SB_HEAD), lambda b, h, q: (b, vb + h)),
            pl.BlockSpec((1, SB_HEAD), lambda b, h, q: (0, 0)),
            pl.BlockSpec((1, SB_HEAD), lambda b, h, q: (0, 0)),
        ],
        out_specs=pl.BlockSpec((SB_BQ, SB_HEAD), lambda b, h, q: (b * nq + q, h)),
        out_shape=jax.ShapeDtypeStruct((bsz * seq, SB_W), BF16),
        scratch_shapes=[pltpu.VMEM((seq, SB_HEAD), BF16), pltpu.VMEM((seq, SB_HEAD), BF16)],
        compiler_params=_cparams(("parallel", "parallel", "arbitrary")),
        name="stick_breaking",
    )(p, p, p, qg.reshape(1, -1), kg.reshape(1, -1))


PREP_HALO = 8


def _head_ones():
    r = lax.broadcasted_iota(jnp.int32, (PAIR, PAIR), 0) // RWKV_HEAD
    c = lax.broadcasted_iota(jnp.int32, (PAIR, PAIR), 1) // RWKV_HEAD
    return (r == c).astype(BF16)


def _rwkv_prep_kernel(p_ref, lora_ref, hp_ref, hlora_ref, mu_ref, mul_ref, w0_ref, wup_ref, a0_ref,
                      aup_ref, gup_ref, kk_ref, ka_ref, rk_ref,
                      r_o, k_o, v_o, lw_o, a_o, b_o, bonus_o, g_o, *, first):
    def shifted(x_ref, h_ref, m_ref, c0, c1):
        x = x_ref[:, c0:c1]
        prev = pltpu.roll(x, 1, 0)
        row = lax.broadcasted_iota(jnp.int32, x.shape, 0)
        last = h_ref[PREP_HALO - 1:PREP_HALO, c0:c1]
        last = jnp.where(first, jnp.zeros_like(last), last)
        prev = jnp.where(row == 0, last, prev)
        return x + m_ref[:, c0:c1] * (prev - x)

    lo = shifted(lora_ref, hlora_ref, mul_ref, 0, LORA_COLS)
    w_lo = lo[:, 0:LORA_PAD]
    a_lo = lo[:, LORA_PAD:2 * LORA_PAD]
    g_lo = lo[:, 2 * LORA_PAD:]
    ones = _head_ones()

    for c0 in range(0, RWKV_W, 512):
        c1 = c0 + 512
        r = shifted(p_ref, hp_ref, mu_ref, c0, c1)
        k = shifted(p_ref, hp_ref, mu_ref, RWKV_W + c0, RWKV_W + c1)
        v = shifted(p_ref, hp_ref, mu_ref, 2 * RWKV_W + c0, 2 * RWKV_W + c1)
        y = w0_ref[:, c0:c1] + _dot(jnp.tanh(w_lo), wup_ref[:, c0:c1])
        log_w = -_softplus(-y) - 0.5
        lw_o[:, c0:c1] = -jnp.exp(log_w)
        iclr = jax.nn.sigmoid(a0_ref[:, c0:c1] + _dot(a_lo, aup_ref[:, c0:c1]))
        g_o[:, c0:c1] = _dot(jax.nn.sigmoid(g_lo), gup_ref[:, c0:c1])
        kk = k * kk_ref[:, c0:c1]
        kt = k * (1.0 + (iclr - 1.0) * ka_ref[:, c0:c1])
        rkr = r * kt * rk_ref[:, c0:c1]
        for t0 in range(0, 512, PAIR):
            t1 = t0 + PAIR
            kk_t = kk[:, t0:t1]
            ss = _dot_x2(kk_t * kk_t, ones)
            kk_n = kk_t * lax.rsqrt(jnp.maximum(ss, 1e-24))
            a_o[:, c0 + t0:c0 + t1] = -kk_n
            b_o[:, c0 + t0:c0 + t1] = kk_n * iclr[:, t0:t1]
            bonus_o[:, c0 + t0:c0 + t1] = _dot(rkr[:, t0:t1], ones) * v[:, t0:t1]
        r_o[:, c0:c1] = r
        k_o[:, c0:c1] = kt
        v_o[:, c0:c1] = v


def _rwkv_prep(p, mu, mul, w0, wup, a0, aup, gup, kk, ka, rk, seq, tm=128):
    m = p.shape[0]
    lb = COL_LORA // LORA_COLS
    hmap = lambda i: jnp.maximum(i * (tm // PREP_HALO) - 1, 0)
    row = lambda a: a.reshape(1, -1)
    vec = pl.BlockSpec((1, RWKV_W), lambda i: (0, 0))
    out = jax.ShapeDtypeStruct((m, RWKV_W), F32)
    ospec = pl.BlockSpec((tm, RWKV_W), lambda i: (i, 0))
    return pl.pallas_call(
        functools.partial(_rwkv_prep_kernel, tm=tm, seq=seq),
        grid=(m // tm,),
        in_specs=[
            pl.BlockSpec((tm, 3 * RWKV_W), lambda i: (i, 0)),
            pl.BlockSpec((tm, LORA_COLS), lambda i: (i, lb)),
            pl.BlockSpec((PREP_HALO, 3 * RWKV_W), lambda i: (hmap(i), 0)),
            pl.BlockSpec((PREP_HALO, LORA_COLS), lambda i: (hmap(i), lb)),
            pl.BlockSpec((1, 3 * RWKV_W), lambda i: (0, 0)),
            pl.BlockSpec((1, LORA_COLS), lambda i: (0, 0)),
            vec,
            pl.BlockSpec((LORA_PAD, RWKV_W), lambda i: (0, 0)),
            vec,
            pl.BlockSpec((LORA_PAD, RWKV_W), lambda i: (0, 0)),
            pl.BlockSpec((GATE_LORA, RWKV_W), lambda i: (0, 0)),
            vec, vec, vec,
        ],
        out_specs=[ospec] * 8,
        out_shape=[out] * 8,
        compiler_params=_cparams(("parallel",)),
        name="rwkv_prep",
    )(p, p, p, p, row(mu), row(mul), row(w0), wup, row(a0), aup, gup, row(kk), row(ka), row(rk))


SCAN_PAIRS = 2
GN_ROWS = 256


def _rwkv_scan_kernel(r_ref, k_ref, v_ref, lw_ref, a_ref, b_ref, bonus_ref, g_ref, gng_ref, gnb_ref,
                      o_ref, s_ref, rq_s, oin_s, tr_s, dl_s, *, tblk):
    C = RWKV_C
    t_id = pl.program_id(2)

    @pl.when(t_id == 0)
    def _():
        s_ref[...] = jnp.zeros_like(s_ref)

    lane = lax.broadcasted_iota(jnp.int32, (1, PAIR), 1)
    m0 = (lane < RWKV_HEAD).astype(F32)
    m1 = 1.0 - m0
    ti = lax.broadcasted_iota(jnp.int32, (C, C), 0)
    ii = lax.broadcasted_iota(jnp.int32, (C, C), 1)
    tri_incl = (ii <= ti).astype(BF16)
    t2 = lax.broadcasted_iota(jnp.int32, (C, 2 * C), 0)
    i2 = lax.broadcasted_iota(jnp.int32, (C, 2 * C), 1) % C
    strict2 = (i2 < t2).astype(F32)
    incl2 = (i2 <= t2).astype(F32)
    rr = lax.broadcasted_iota(jnp.int32, (PAIR, PAIR), 0)
    cc = lax.broadcasted_iota(jnp.int32, (PAIR, PAIR), 1)
    bdmask = (rr // RWKV_HEAD == cc // RWKV_HEAD).astype(F32)
    eye = (rr == cc).astype(F32)
    ones = _head_ones()

    def stack_heads(x):
        return jnp.concatenate([x * m0, x * m1], axis=0)

    def chunk_pair(c, pi):
        sl = pl.ds(pl.multiple_of(c * C, C), C)
        ln = slice(pi * PAIR, (pi + 1) * PAIR)
        lw = lw_ref[sl, ln]
        r, k, v, a, b = r_ref[sl, ln], k_ref[sl, ln], v_ref[sl, ln], a_ref[sl, ln], b_ref[sl, ln]
        cum = _x3_dot(tri_incl, lw)
        tot = cum[C - 1:C, :]
        e_neg = jnp.exp(-cum)
        e_end = jnp.exp(tot - cum)
        at = a * jnp.exp(cum - lw)
        rt = r * jnp.exp(cum)
        bh, kh = b * e_neg, k * e_neg
        bend, kend = b * e_end, k * e_end
        gram = _dot_nt(jnp.concatenate([at, rt], axis=0),
                       jnp.concatenate([stack_heads(bh), stack_heads(kh)], axis=0))
        m_ab = gram[0:C, 0:2 * C] * strict2
        m_ak = gram[0:C, 2 * C:] * strict2
        q_b = gram[C:, 0:2 * C] * incl2
        q_k = gram[C:, 2 * C:] * incl2
        vv = stack_heads(v)
        u_rhs = _dot(m_ak, vv)
        bd = stack_heads(m_ab)
        t_inv = eye + bd
        pw = bd
        for _ in range(5):
            pw = _dot(pw, pw)
            t_inv = t_inv + _dot(t_inv, pw)
        t_ls = t_inv[0:C, :] + t_inv[C:, :]
        sol = _dot(t_ls, jnp.concatenate(
            [jnp.concatenate([at * m0, u_rhs * m0], axis=1),
             jnp.concatenate([at * m1, u_rhs * m1], axis=1)], axis=0))
        wa, u0 = sol[:, 0:PAIR], sol[:, PAIR:]
        y = _dot(q_b, jnp.concatenate(
            [jnp.concatenate([wa * m0, u0 * m0], axis=1),
             jnp.concatenate([wa * m1, u0 * m1], axis=1)], axis=0))
        rq_s[sl, ln] = rt + y[:, 0:PAIR]
        oin_s[sl, ln] = y[:, PAIR:] + _dot(q_k, vv)
        tr_s[pi, c] = _dot_tn(wa, bend) * bdmask + eye * jnp.exp(tot)
        dl_s[pi, c] = (_dot_tn(u0, bend) + _dot_tn(v, kend)) * bdmask

    def chunk(c, carry):
        for pi in range(SCAN_PAIRS):
            chunk_pair(c, pi)
        return carry

    lax.fori_loop(0, tblk // C, chunk, 0)

    def scan(c, carry):
        sl = pl.ds(pl.multiple_of(c * C, C), C)
        for pi in range(SCAN_PAIRS):
            ln = slice(pi * PAIR, (pi + 1) * PAIR)
            s = s_ref[pi]
            oin_s[sl, ln] = _dot_nt(rq_s[sl, ln], s) + oin_s[sl, ln]
            s_ref[pi] = _dot(s, tr_s[pi, c]) + dl_s[pi, c]
        return carry

    lax.fori_loop(0, tblk // C, scan, 0)

    for pi in range(SCAN_PAIRS):
        ln = slice(pi * PAIR, (pi + 1) * PAIR)
        for r0 in range(0, tblk, GN_ROWS):
            rows = slice(r0, r0 + GN_ROWS)
            o = oin_s[rows, ln]
            mean = _dot_x3(o, ones) * (1.0 / RWKV_HEAD)
            d = o - mean
            var = _dot_x3(d * d, ones) * (1.0 / RWKV_HEAD)
            on = d * lax.rsqrt(var + GN_EPS) * gng_ref[:, ln] + gnb_ref[:, ln]
            o_ref[rows, ln] = ((on + bonus_ref[rows, ln]) * g_ref[rows, ln]).astype(o_ref.dtype)


def _rwkv_scan(r, k, v, lw, a, b, bonus, g, gng, gnb, bsz, seq, tblk=512):
    nt = seq // tblk
    width = SCAN_PAIRS * PAIR
    tile = pl.BlockSpec((tblk, width), lambda bi, h, t: (bi * nt + t, h))
    vec = pl.BlockSpec((1, width), lambda bi, h, t: (0, h))
    nch = tblk // RWKV_C
    return pl.pallas_call(
        functools.partial(_rwkv_scan_kernel, tblk=tblk),
        grid=(bsz, RWKV_W // width, nt),
        in_specs=[tile] * 8 + [vec, vec],
        out_specs=tile,
        out_shape=jax.ShapeDtypeStruct((bsz * seq, RWKV_W), BF16),
        scratch_shapes=[pltpu.VMEM((SCAN_PAIRS, PAIR, PAIR), F32),
                        pltpu.VMEM((tblk, width), F32), pltpu.VMEM((tblk, width), F32),
                        pltpu.VMEM((SCAN_PAIRS, nch, PAIR, PAIR), F32),
                        pltpu.VMEM((SCAN_PAIRS, nch, PAIR, PAIR), F32)],
        compiler_params=_cparams(("parallel", "parallel", "arbitrary")),
        name="rwkv_scan",
    )(r, k, v, lw, a, b, bonus, g, gng.reshape(1, -1), gnb.reshape(1, -1))


QUAD_HEADS = 2
QUAD = QUAD_HEADS * RWKV_HEAD
SCAN_GROUPS = 16
SCAN_ROWS = 256


def _rwkv_quad_kernel(r_ref, k_ref, v_ref, lw_ref, a_ref, b_ref, bonus_ref, g_ref, gng_ref, gnb_ref,
                      o_ref, s_ref, rq_s, oin_s, tr_s, dl_s, *, tblk, t_id):
    C = RWKV_C
    HC = QUAD_HEADS * C
    G = range(SCAN_GROUPS)

    @pl.when(t_id == 0)
    def _():
        s_ref[...] = jnp.zeros_like(s_ref)

    lane = lax.broadcasted_iota(jnp.int32, (1, QUAD), 1) // RWKV_HEAD
    hm = [(lane == h).astype(F32) for h in range(QUAD_HEADS)]
    ti = lax.broadcasted_iota(jnp.int32, (C, C), 0)
    ii = lax.broadcasted_iota(jnp.int32, (C, C), 1)
    tri_incl = (ii <= ti).astype(BF16)
    t4 = lax.broadcasted_iota(jnp.int32, (C, HC), 0)
    i4 = lax.broadcasted_iota(jnp.int32, (C, HC), 1) % C
    strict4 = (i4 < t4).astype(F32)
    incl4 = (i4 <= t4).astype(F32)
    rr = lax.broadcasted_iota(jnp.int32, (QUAD, QUAD), 0)
    cc = lax.broadcasted_iota(jnp.int32, (QUAD, QUAD), 1)
    bdmask = (rr // RWKV_HEAD == cc // RWKV_HEAD).astype(F32)
    eye = (rr == cc).astype(F32)
    ones = (rr // RWKV_HEAD == cc // RWKV_HEAD).astype(BF16)

    def stack_heads(x):
        return jnp.concatenate([x * m for m in hm], axis=0)

    def stack_heads2(x, y):
        return jnp.concatenate([jnp.concatenate([x * m, y * m], axis=1) for m in hm], axis=0)

    def chunk(c, carry):
        sl = pl.ds(pl.multiple_of(c * C, C), C)
        lns = [slice(g * QUAD, (g + 1) * QUAD) for g in G]
        lw = [lw_ref[sl, ln] for ln in lns]
        cum = [_x3_dot(tri_incl, x) for x in lw]
        tot = [x[C - 1:C, :] for x in cum]
        e_neg = [jnp.exp(-x) for x in cum]
        e_end = [jnp.exp(t - x) for t, x in zip(tot, cum)]
        at = [a_ref[sl, ln] * jnp.exp(x - w) for ln, x, w in zip(lns, cum, lw)]
        rt = [r_ref[sl, ln] * jnp.exp(x) for ln, x in zip(lns, cum)]
        b = [b_ref[sl, ln] for ln in lns]
        k = [k_ref[sl, ln] for ln in lns]
        v = [v_ref[sl, ln] for ln in lns]
        gram = [_dot_nt(jnp.concatenate([at[g], rt[g]], axis=0),
                        jnp.concatenate([stack_heads(b[g] * e_neg[g]), stack_heads(k[g] * e_neg[g])], axis=0))
                for g in G]
        m_ak = [x[0:C, HC:] * strict4 for x in gram]
        q_b = [x[C:, 0:HC] * incl4 for x in gram]
        q_k = [x[C:, HC:] * incl4 for x in gram]
        vv = [stack_heads(x) for x in v]
        u_rhs = [_dot(m_ak[g], vv[g]) for g in G]
        pw = [stack_heads(x[0:C, 0:HC] * strict4) for x in gram]
        t_inv = [eye + x for x in pw]
        pw = [_dot(x, x) for x in pw]
        for _ in range(4):
            both = [_dot(jnp.concatenate([x, t], axis=0), x) for t, x in zip(t_inv, pw)]
            pw = [x[0:HC, :] for x in both]
            t_inv = [t + x[HC:, :] for t, x in zip(t_inv, both)]
        t_inv = [t + _dot(t, x) for t, x in zip(t_inv, pw)]
        t_ls = [sum(t[h * C:(h + 1) * C, :] for h in range(1, QUAD_HEADS)) + t[0:C, :] for t in t_inv]
        sol = [_dot(t_ls[g], stack_heads2(at[g], u_rhs[g])) for g in G]
        wa = [x[:, 0:QUAD] for x in sol]
        u0 = [x[:, QUAD:] for x in sol]
        y = [_dot(q_b[g], stack_heads2(wa[g], u0[g])) for g in G]
        o_in2 = [_dot(q_k[g], vv[g]) for g in G]
        bend = [b[g] * e_end[g] for g in G]
        tr = [_dot_tn(wa[g], bend[g]) for g in G]
        dl = [_dot_tn(jnp.concatenate([u0[g], v[g]], axis=0),
                      jnp.concatenate([bend[g], k[g] * e_end[g]], axis=0)) for g in G]
        for g in G:
            rq_s[sl, lns[g]] = rt[g] + y[g][:, 0:QUAD]
            oin_s[sl, lns[g]] = y[g][:, QUAD:] + o_in2[g]
            tr_s[g, c] = tr[g] * bdmask + eye * jnp.exp(tot[g])
            dl_s[g, c] = dl[g] * bdmask
        return carry

    lax.fori_loop(0, tblk // C, chunk, 0)

    def scan(c, carry):
        sl = pl.ds(pl.multiple_of(c * C, C), C)
        for g in G:
            ln = slice(g * QUAD, (g + 1) * QUAD)
            s = s_ref[g]
            oin_s[sl, ln] = _dot_nt(rq_s[sl, ln], s) + oin_s[sl, ln]
            s_ref[g] = _dot(s, tr_s[g, c]) + dl_s[g, c]
        return carry

    lax.fori_loop(0, tblk // C, scan, 0)

    for g in G:
        ln = slice(g * QUAD, (g + 1) * QUAD)
        for r0 in range(0, tblk, GN_ROWS):
            rows = slice(r0, r0 + GN_ROWS)
            o = oin_s[rows, ln]
            mean = _dot_x3(o, ones) * (1.0 / RWKV_HEAD)
            d = o - mean
            var = _dot_x2(d * d, ones) * (1.0 / RWKV_HEAD)
            on = d * lax.rsqrt(var + GN_EPS) * gng_ref[:, ln] + gnb_ref[:, ln]
            o_ref[rows, ln] = ((on + bonus_ref[rows, ln]) * g_ref[rows, ln]).astype(o_ref.dtype)


def _rwkv_quad_scan(r, k, v, lw, a, b, bonus, g, gng, gnb, bsz, seq, tblk=512):
    nt = seq // tblk
    width = SCAN_GROUPS * QUAD
    tile = pl.BlockSpec((tblk, width), lambda bi, h, t: (bi * nt + t, h))
    vec = pl.BlockSpec((1, width), lambda bi, h, t: (0, h))
    nch = tblk // RWKV_C
    return pl.pallas_call(
        functools.partial(_rwkv_quad_kernel, tblk=tblk),
        grid=(bsz, RWKV_W // width, nt),
        in_specs=[tile] * 8 + [vec, vec],
        out_specs=tile,
        out_shape=jax.ShapeDtypeStruct((bsz * seq, RWKV_W), BF16),
        scratch_shapes=[pltpu.VMEM((SCAN_GROUPS, QUAD, QUAD), F32),
                        pltpu.VMEM((tblk, width), F32), pltpu.VMEM((tblk, width), F32),
                        pltpu.VMEM((SCAN_GROUPS, nch, QUAD, QUAD), F32),
                        pltpu.VMEM((SCAN_GROUPS, nch, QUAD, QUAD), F32)],
        compiler_params=_cparams(("parallel", "parallel", "arbitrary")),
        name="rwkv_scan",
    )(r, k, v, lw, a, b, bonus, g, gng.reshape(1, -1), gnb.reshape(1, -1))


def _rwkv_fused_kernel(p_ref, lora_ref, hp_ref, hlora_ref, mu_ref, mul_ref, w0_ref, wup_ref, a0_ref,
                       aup_ref, gup_ref, kk_ref, ka_ref, rk_ref, gng_ref, gnb_ref, o_ref,
                       r_s, k_s, v_s, lw_s, a_s, b_s, bonus_s, g_s, s_ref, rq_s, oin_s, tr_s, dl_s, *, tblk):
    t_id = pl.program_id(1)
    _rwkv_prep_kernel(p_ref, lora_ref, hp_ref, hlora_ref, mu_ref, mul_ref, w0_ref, wup_ref, a0_ref,
                      aup_ref, gup_ref, kk_ref, ka_ref, rk_ref,
                      r_s, k_s, v_s, lw_s, a_s, b_s, bonus_s, g_s, first=t_id == 0)
    _rwkv_quad_kernel(r_s, k_s, v_s, lw_s, a_s, b_s, bonus_s, g_s, gng_ref, gnb_ref,
                      o_ref, s_ref, rq_s, oin_s, tr_s, dl_s, tblk=tblk, t_id=t_id)


def _rwkv_mix(p, mu, mul, w0, wup, a0, aup, gup, kk, ka, rk, gng, gnb, bsz, seq, tblk):
    assert SCAN_GROUPS * QUAD == RWKV_W and RWKV_C == RWKV_HEAD
    nt = seq // tblk
    nch = tblk // RWKV_C
    lb = COL_LORA // LORA_COLS
    blk = lambda b, t: b * nt + t
    hmap = lambda b, t: jnp.maximum(blk(b, t) * (tblk // PREP_HALO) - 1, 0)
    row = lambda a: a.reshape(1, -1)
    vec = pl.BlockSpec((1, RWKV_W), lambda b, t: (0, 0))
    full = lambda r, c: pl.BlockSpec((r, c), lambda b, t: (0, 0))
    act = pltpu.VMEM((tblk, RWKV_W), F32)
    return pl.pallas_call(
        functools.partial(_rwkv_fused_kernel, tblk=tblk),
        grid=(bsz, nt),
        in_specs=[
            pl.BlockSpec((tblk, 3 * RWKV_W), lambda b, t: (blk(b, t), 0)),
            pl.BlockSpec((tblk, LORA_COLS), lambda b, t: (blk(b, t), lb)),
            pl.BlockSpec((PREP_HALO, 3 * RWKV_W), lambda b, t: (hmap(b, t), 0)),
            pl.BlockSpec((PREP_HALO, LORA_COLS), lambda b, t: (hmap(b, t), lb)),
            full(1, 3 * RWKV_W), full(1, LORA_COLS),
            vec, full(LORA_PAD, RWKV_W), vec, full(LORA_PAD, RWKV_W), full(GATE_LORA, RWKV_W),
            vec, vec, vec, vec, vec,
        ],
        out_specs=pl.BlockSpec((tblk, RWKV_W), lambda b, t: (blk(b, t), 0)),
        out_shape=jax.ShapeDtypeStruct((bsz * seq, RWKV_W), BF16),
        scratch_shapes=[act] * 8 + [
            pltpu.VMEM((SCAN_GROUPS, QUAD, QUAD), F32), act, act,
            pltpu.VMEM((SCAN_GROUPS, nch, QUAD, QUAD), F32),
            pltpu.VMEM((SCAN_GROUPS, nch, QUAD, QUAD), F32)],
        compiler_params=_cparams(("parallel", "arbitrary")),
        name="rwkv_mix",
    )(p, p, p, p, row(mu), row(mul), row(w0), wup, row(a0), aup, gup, row(kk), row(ka), row(rk),
      row(gng), row(gnb))


def _relayout_cols(w, axis):
    def sl(a, b):
        idx = [slice(None)] * w.ndim
        idx[axis] = slice(a, b)
        return w[tuple(idx)]
    pad_shape = list(w.shape)
    pad_shape[axis] = LORA_PAD - DECAY_LORA
    z = jnp.zeros(pad_shape, w.dtype)
    lo = 3 * RWKV_W
    rest = lo + DECAY_LORA + ICLR_LORA + GATE_LORA
    return jnp.concatenate(
        [sl(0, lo), sl(rest, w.shape[axis]), sl(lo, lo + DECAY_LORA), z,
         sl(lo + DECAY_LORA, lo + DECAY_LORA + ICLR_LORA), z,
         sl(lo + DECAY_LORA + ICLR_LORA, rest)], axis=axis)


def _pad_rows(w):
    return jnp.concatenate([w, jnp.zeros((LORA_PAD - w.shape[0], w.shape[1]), w.dtype)], axis=0)


def _layer(x, bsz, seq, layer, w_in, w_out, attn_norm, ffn_norm, mu, w0, w_up, a0, a_up, g_up, k_k, k_a, r_k,
           gn_g, gn_b, conv_w, conv_b, ln_g, ln_b, q_norm, k_norm, ffn_up, ffn_conv, ffn_down, tm_mm=1024):
    h = _rmsnorm(x, attn_norm)
    p = _matmul([h], w_in, layer, tm=tm_mm, name="proj_in")
    mu_p, mu_l = mu[:3 * RWKV_W], mu[COL_LORA:]
    y_a = _rwkv_mix(p, mu_p, mu_l, w0, w_up, a0, a_up, g_up, k_k, k_a, r_k.reshape(-1), gn_g, gn_b,
                    bsz, seq, tblk=min(SCAN_ROWS, seq))
    y_b = _conformer(p, conv_w, conv_b, ln_g, ln_b, seq)
    y_c = _stick_breaking(p, q_norm, k_norm, bsz, seq)
    x = _matmul([y_a, y_b, y_c], w_out, layer, residual=x, tm=tm_mm, name="proj_out")
    h = _rmsnorm(x, ffn_norm)
    act = _ffn_up(h, ffn_up, layer, ffn_conv, seq, tm=tm_mm)
    return _matmul([act], ffn_down, layer, residual=x, tm=min(512, tm_mm), name="ffn_down")


def kernel(x, w_in, w_out, attn_norm, ffn_norm, rwkv_mu, rwkv_w0, rwkv_w_up, rwkv_a0, rwkv_a_up,
           rwkv_g_up, rwkv_k_k, rwkv_k_a, rwkv_r_k, rwkv_gn_g, rwkv_gn_b, conv_w, conv_b, conv_ln_g,
           conv_ln_b, sb_q_norm, sb_k_norm, ffn_up, ffn_conv, ffn_down):
    bsz, seq, d = x.shape
    x = x.reshape(bsz * seq, d)
    w_in_b = _relayout_cols(w_in.astype(BF16), 2)
    w_out_b, ffn_up_b, ffn_down_b = w_out.astype(BF16), ffn_up.astype(BF16), ffn_down.astype(BF16)
    for l in range(w_in.shape[0]):
        mu_full = jnp.concatenate([rwkv_mu[l], jnp.zeros((w_in.shape[2] - rwkv_mu.shape[1],), F32)])
        mu_re = _relayout_cols(mu_full, 0)
        x = _layer(
            x, bsz, seq, l, w_in_b, w_out_b, attn_norm[l], ffn_norm[l],
            mu_re, rwkv_w0[l], _pad_rows(rwkv_w_up[l]).astype(BF16), rwkv_a0[l],
            _pad_rows(rwkv_a_up[l]).astype(BF16), rwkv_g_up[l].astype(BF16), rwkv_k_k[l], rwkv_k_a[l],
            rwkv_r_k[l], rwkv_gn_g[l], rwkv_gn_b[l], conv_w[l], conv_b[l], conv_ln_g[l], conv_ln_b[l],
            sb_q_norm[l], sb_k_norm[l], ffn_up_b, ffn_conv[l], ffn_down_b,
            tm_mm=min(1024, seq))
    return x.reshape(bsz, seq, d)
```

```python
import functools

import jax
import jax.numpy as jnp
from jax import lax
from jax.experimental import pallas as pl
from jax.experimental.pallas import tpu as pltpu

F32 = jnp.float32
BF16 = jnp.bfloat16

D_MODEL = 4096
DEPTH = 4
RWKV_HEAD = 64
RWKV_W = 2048
RWKV_HEADS = RWKV_W // RWKV_HEAD
DECAY_LORA = 96
ICLR_LORA = 96
GATE_LORA = 256
CONV_W = 1024
CONV_K = 31
SB_HEAD = 128
SB_W = 1024
SB_HEADS = SB_W // SB_HEAD
D_FF = 2 * D_MODEL
FFN_CONV_K = 3
NORM_EPS = 1e-6
LN_EPS = 1e-5
GN_EPS = RWKV_HEAD * 1e-5

LANES = 128
SUBLANES = 8
LORA_PAD = 128
COL_RKV = 0
COL_CVAL = 3 * RWKV_W
COL_CGATE = COL_CVAL + CONV_W
COL_SBQ = COL_CGATE + CONV_W
COL_SBK = COL_SBQ + SB_W
COL_SBV = COL_SBK + SB_W
COL_LORA = COL_SBV + SB_W
LORA_COLS = 2 * LORA_PAD + GATE_LORA
PROJ_PAD = COL_LORA + LORA_COLS
VMEM_LIMIT = 56 * 1024 * 1024

RWKV_C = 64
PAIR = 2 * RWKV_HEAD


def _cparams(sem):
    return pltpu.CompilerParams(dimension_semantics=sem, vmem_limit_bytes=VMEM_LIMIT)


def _dot(a, b):
    return jnp.dot(a.astype(BF16), b.astype(BF16), preferred_element_type=F32)


def _dot_nt(a, b):
    return lax.dot_general(a.astype(BF16), b.astype(BF16), (((1,), (1,)), ((), ())),
                           preferred_element_type=F32)


def _dot_tn(a, b):
    return lax.dot_general(a.astype(BF16), b.astype(BF16), (((0,), (0,)), ((), ())),
                           preferred_element_type=F32)


def _split3(x):
    hi = x.astype(BF16)
    r1 = x - hi.astype(F32)
    mid = r1.astype(BF16)
    lo = (r1 - mid.astype(F32)).astype(BF16)
    return hi, mid, lo


def _dot_x3(a, b_exact):
    hi, mid, lo = _split3(a)
    b = b_exact.astype(BF16)
    return (jnp.dot(hi, b, preferred_element_type=F32)
            + jnp.dot(mid, b, preferred_element_type=F32)
            + jnp.dot(lo, b, preferred_element_type=F32))


def _x3_dot(a_exact, b):
    hi, mid, lo = _split3(b)
    a = a_exact.astype(BF16)
    return (jnp.dot(a, hi, preferred_element_type=F32)
            + jnp.dot(a, mid, preferred_element_type=F32)
            + jnp.dot(a, lo, preferred_element_type=F32))


def _dot_x2(a, b_exact):
    hi = a.astype(BF16)
    lo = (a - hi.astype(F32)).astype(BF16)
    b = b_exact.astype(BF16)
    return jnp.dot(hi, b, preferred_element_type=F32) + jnp.dot(lo, b, preferred_element_type=F32)


_SB_SUFFIX_DOT = _dot


def _dot_hi(a, b):
    ah = a.astype(BF16)
    al = (a - ah.astype(F32)).astype(BF16)
    bh = b.astype(BF16)
    bl = (b - bh.astype(F32)).astype(BF16)
    return (jnp.dot(ah, bh, preferred_element_type=F32)
            + jnp.dot(al, bh, preferred_element_type=F32)
            + jnp.dot(ah, bl, preferred_element_type=F32))


def _neg_abs(x):
    bits = lax.bitcast_convert_type(x, jnp.int32) | jnp.int32(-2 ** 31)
    return lax.bitcast_convert_type(bits, F32)


def _softplus(x):
    return jnp.maximum(x, 0.0) + jnp.log(1.0 + jnp.exp(_neg_abs(x)))


def _rmsnorm_kernel(x_ref, g_ref, o_ref):
    x = x_ref[...]
    ms = jnp.mean(x * x, axis=-1, keepdims=True)
    o_ref[...] = (x * lax.rsqrt(ms + NORM_EPS) * g_ref[...]).astype(o_ref.dtype)


def _rmsnorm(x, g, tm=256):
    m, d = x.shape
    return pl.pallas_call(
        _rmsnorm_kernel,
        grid=(m // tm,),
        in_specs=[pl.BlockSpec((tm, d), lambda i: (i, 0)),
                  pl.BlockSpec((1, d), lambda i: (0, 0))],
        out_specs=pl.BlockSpec((tm, d), lambda i: (i, 0)),
        out_shape=jax.ShapeDtypeStruct((m, d), BF16),
        compiler_params=_cparams(("parallel",)),
        name="rmsnorm",
    )(x, g.reshape(1, d))


def _mm_kernel(*refs, n_lhs, has_res):
    o_ref = refs[-1]
    acc = None
    for a_ref, w_ref in zip(refs[:n_lhs], refs[n_lhs:2 * n_lhs]):
        d = jnp.dot(a_ref[...], w_ref[...], preferred_element_type=F32)
        acc = d if acc is None else acc + d
    if has_res:
        acc = acc + refs[2 * n_lhs][...]
    o_ref[...] = acc.astype(o_ref.dtype)


def _mm_nt_kernel(a_ref, w_ref, o_ref):
    o_ref[...] = lax.dot_general(a_ref[...], w_ref[...], (((1,), (1,)), ((), ())),
                                 preferred_element_type=F32)


def _matmul_nt(a, w_t, layer, tm=1024, tn=512, name="matmul_nt"):
    m, k = a.shape
    n = w_t.shape[1]
    return pl.pallas_call(
        _mm_nt_kernel,
        grid=(m // tm, n // tn),
        in_specs=[pl.BlockSpec((tm, k), lambda i, j: (i, 0)),
                  pl.BlockSpec((pl.Squeezed(), tn, k), lambda i, j: (layer, j, 0))],
        out_specs=pl.BlockSpec((tm, tn), lambda i, j: (i, j)),
        out_shape=jax.ShapeDtypeStruct((m, n), F32),
        compiler_params=_cparams(("parallel", "arbitrary")),
        name=name,
    )(a, w_t)


def _matmul(lhs_list, w, layer, residual=None, tm=1024, tn=512, name="matmul"):
    m = lhs_list[0].shape[0]
    n = w.shape[2]
    in_specs, args = [], []
    for a in lhs_list:
        in_specs.append(pl.BlockSpec((tm, a.shape[1]), lambda i, j: (i, 0)))
        args.append(a)
    row = 0
    for a in lhs_list:
        k = a.shape[1]
        assert row % k == 0
        in_specs.append(pl.BlockSpec((pl.Squeezed(), k, tn),
                                     functools.partial(lambda i, j, rb: (layer, rb, j), rb=row // k)))
        args.append(w)
        row += k
    assert row == w.shape[1]
    if residual is not None:
        in_specs.append(pl.BlockSpec((tm, tn), lambda i, j: (i, j)))
        args.append(residual)
    return pl.pallas_call(
        functools.partial(_mm_kernel, n_lhs=len(lhs_list), has_res=residual is not None),
        grid=(m // tm, n // tn),
        in_specs=in_specs,
        out_specs=pl.BlockSpec((tm, tn), lambda i, j: (i, j)),
        out_shape=jax.ShapeDtypeStruct((m, n), F32),
        compiler_params=_cparams(("parallel", "arbitrary")),
        name=name,
    )(*args)


FFN_CARRY = 8
FFN_SUB = 256


def _ffn_up_kernel(h_ref, wg_ref, wv_ref, cg_ref, cv_ref, o_ref, *scratch, tm, tn, seq):
    i = pl.program_id(1)
    first = (i * tm) % seq == 0
    nsub = tn // FFN_SUB
    sg, sv = scratch[:nsub], scratch[nsub:]

    @pl.when(first)
    def _():
        for s in scratch:
            s[0:FFN_CARRY, :] = jnp.zeros((FFN_CARRY, FFN_SUB), F32)

    @pl.when(jnp.logical_not(first))
    def _():
        for s in scratch:
            s[0:FFN_CARRY, :] = s[tm:tm + FFN_CARRY, :]

    def up(w_ref, s, n):
        s[FFN_CARRY:FFN_CARRY + tm, :] = jnp.dot(h_ref[...], w_ref[:, n * FFN_SUB:(n + 1) * FFN_SUB],
                                                 preferred_element_type=F32)

    def conv(c_ref, s, n):
        acc = None
        for k in range(FFN_CONV_K):
            off = FFN_CARRY - (FFN_CONV_K - 1) + k
            term = c_ref[k:k + 1, n * FFN_SUB:(n + 1) * FFN_SUB] * s[off:off + tm, :]
            acc = term if acc is None else acc + term
        return acc

    def gate(n):
        g = conv(cg_ref, sg[n], n)
        v = conv(cv_ref, sv[n], n)
        silu = 0.5 * g * (1.0 + jnp.tanh(0.5 * g))
        o_ref[:, n * FFN_SUB:(n + 1) * FFN_SUB] = (silu * v).astype(o_ref.dtype)

    up(wg_ref, sg[0], 0)
    up(wv_ref, sv[0], 0)
    for n in range(1, nsub):
        up(wg_ref, sg[n], n)
        gate(n - 1)
        up(wv_ref, sv[n], n)
    gate(nsub - 1)


def _ffn_up(h, w_up, layer, conv, seq, tm=1024, tn=512):
    m, d = h.shape
    nj = D_FF // tn
    return pl.pallas_call(
        functools.partial(_ffn_up_kernel, tm=tm, tn=tn, seq=seq),
        grid=(nj, m // tm),
        in_specs=[
            pl.BlockSpec((tm, d), lambda j, i: (i, 0)),
            pl.BlockSpec((pl.Squeezed(), d, tn), lambda j, i: (layer, 0, j)),
            pl.BlockSpec((pl.Squeezed(), d, tn), lambda j, i: (layer, 0, j + nj)),
            pl.BlockSpec((FFN_CONV_K, tn), lambda j, i: (0, j)),
            pl.BlockSpec((FFN_CONV_K, tn), lambda j, i: (0, j + nj)),
        ],
        out_specs=pl.BlockSpec((tm, tn), lambda j, i: (i, j)),
        out_shape=jax.ShapeDtypeStruct((m, D_FF), BF16),
        scratch_shapes=[pltpu.VMEM((FFN_CARRY + tm, FFN_SUB), F32)] * (2 * (tn // FFN_SUB)),
        compiler_params=_cparams(("parallel", "arbitrary")),
        name="ffn_up_conv_gate",
    )(h, w_up, w_up, conv, conv)


CONF_HALO = 32
CONF_ROWS = 32
CONF_COLS = 256


def _conformer_kernel(val_ref, gate_ref, hval_ref, hgate_ref, cw_ref, cb_ref, lng_ref, lnb_ref,
                      o_ref, glu_s, u_s, *, tm, seq):
    i = pl.program_id(0)
    first = (i * tm) % seq == 0
    hglu = hval_ref[...] * jax.nn.sigmoid(hgate_ref[...])
    glu_s[0, 0:CONF_HALO, :] = jnp.where(first, jnp.zeros_like(hglu), hglu)
    glu_s[0, CONF_HALO:CONF_HALO + tm, :] = val_ref[...] * jax.nn.sigmoid(gate_ref[...])
    for b in range(1, SUBLANES):
        glu_s[b, SUBLANES:CONF_HALO + tm, :] = glu_s[0, SUBLANES - b:CONF_HALO + tm - b, :]
    for r0 in range(0, tm, CONF_ROWS):
        for c0 in range(0, CONV_W, CONF_COLS):
            acc = jnp.zeros((CONF_ROWS, CONF_COLS), F32) + cb_ref[:, c0:c0 + CONF_COLS]
            for k in range(CONV_K):
                a, b = divmod(CONV_K - 1 - k, SUBLANES)
                start = CONF_HALO + r0 - SUBLANES * a
                acc = acc + cw_ref[k:k + 1, c0:c0 + CONF_COLS] * glu_s[b, start:start + CONF_ROWS,
                                                                   c0:c0 + CONF_COLS]
            u_s[r0:r0 + CONF_ROWS, c0:c0 + CONF_COLS] = acc
    u = u_s[...]
    mean = jnp.mean(u, axis=-1, keepdims=True)
    d = u - mean
    var = jnp.mean(d * d, axis=-1, keepdims=True)
    y = d * lax.rsqrt(var + LN_EPS) * lng_ref[...] + lnb_ref[...]
    o_ref[...] = (y * jax.nn.sigmoid(y)).astype(o_ref.dtype)


def _conformer(p, cw, cb, lng, lnb, seq, tm=256):
    m = p.shape[0]
    vb, gb = COL_CVAL // CONV_W, COL_CGATE // CONV_W
    hmap = lambda i: jnp.maximum(i * (tm // CONF_HALO) - 1, 0)
    return pl.pallas_call(
        functools.partial(_conformer_kernel, tm=tm, seq=seq),
        grid=(m // tm,),
        in_specs=[
            pl.BlockSpec((tm, CONV_W), lambda i: (i, vb)),
            pl.BlockSpec((tm, CONV_W), lambda i: (i, gb)),
            pl.BlockSpec((CONF_HALO, CONV_W), lambda i: (hmap(i), vb)),
            pl.BlockSpec((CONF_HALO, CONV_W), lambda i: (hmap(i), gb)),
            pl.BlockSpec((CONV_K, CONV_W), lambda i: (0, 0)),
            pl.BlockSpec((1, CONV_W), lambda i: (0, 0)),
            pl.BlockSpec((1, CONV_W), lambda i: (0, 0)),
            pl.BlockSpec((1, CONV_W), lambda i: (0, 0)),
        ],
        out_specs=pl.BlockSpec((tm, CONV_W), lambda i: (i, 0)),
        out_shape=jax.ShapeDtypeStruct((m, CONV_W), BF16),
        scratch_shapes=[pltpu.VMEM((SUBLANES, CONF_HALO + tm, CONV_W), F32),
                        pltpu.VMEM((tm, CONV_W), F32)],
        compiler_params=_cparams(("parallel",)),
        name="conformer_conv",
    )(p, p, p, p, cw, cb.reshape(1, -1), lng.reshape(1, -1), lnb.reshape(1, -1))


SB_BQ = 512
SB_BK = 256
SB_PREP_ROWS = 512
SB_GROUP = 8
LOG2E = 1.4426950408889634
LN2 = 0.6931471805599453


def _sb_kernel(q_ref, k_ref, v_ref, qg_ref, kg_ref, o_ref, kn_s, vb_s, *, seq):
    qi = pl.program_id(2)

    @pl.when(qi == 0)
    def _():
        def body(c, carry):
            sl = pl.ds(pl.multiple_of(c * SB_PREP_ROWS, SB_PREP_ROWS), SB_PREP_ROWS)
            k = k_ref[sl, :]
            ms = jnp.mean(k * k, axis=-1, keepdims=True)
            kn_s[sl, :] = (k * lax.rsqrt(ms + NORM_EPS) * kg_ref[...]).astype(BF16)
            vb_s[sl, :] = v_ref[sl, :].astype(BF16)
            return carry
        lax.fori_loop(0, seq // SB_PREP_ROWS, body, 0)

    q = q_ref[...]
    ms = jnp.mean(q * q, axis=-1, keepdims=True)
    q = (q * lax.rsqrt(ms + NORM_EPS) * qg_ref[...] * (SB_HEAD ** -0.5 * LOG2E)).astype(BF16)

    row = lax.broadcasted_iota(jnp.int32, (SB_BK, SB_BK), 0)
    col = lax.broadcasted_iota(jnp.int32, (SB_BK, SB_BK), 1)
    later = (row > col).astype(BF16)
    causal = col < row

    def blocks(qs, k0s, nsub, cs, accs, masked):
        n = range(len(qs))
        sls = [pl.ds(pl.multiple_of(k0, SB_BK), nsub * SB_BK) for k0 in k0s]
        zs = [lax.dot_general(qs[i], kn_s[sls[i], :], (((1,), (1,)), ((), ())), preferred_element_type=F32)
              for i in n]
        sps = [jnp.maximum(z, 0.0) + jnp.log(1.0 + jnp.exp2(_neg_abs(z))) * (1.0 / LN2) for z in zs]
        if masked:
            sps = [jnp.where(causal, sp, 0.0) for sp in sps]
        atts = [[None] * nsub for _ in n]
        for j in reversed(range(nsub)):
            cols = slice(j * SB_BK, (j + 1) * SB_BK)
            sp_j = [sp[:, cols] for sp in sps]
            suffix = [_SB_SUFFIX_DOT(x, later) for x in sp_j]
            att = [jnp.exp2((zs[i][:, cols] - sp_j[i]) - (suffix[i] + cs[i])) for i in n]
            if masked:
                att = [jnp.where(causal, x, 0.0) for x in att]
            for i in n:
                atts[i][j] = att[i].astype(BF16)
            cs = [cs[i] + jnp.sum(sp_j[i], axis=-1, keepdims=True) for i in n]
        accs = [accs[i] + jnp.dot(atts[i][0] if nsub == 1 else jnp.concatenate(atts[i], axis=1),
                                  vb_s[sls[i], :], preferred_element_type=F32) for i in n]
        return cs, accs

    tiles = range(SB_BQ // SB_BK)
    qs = [q[h * SB_BK:(h + 1) * SB_BK, :] for h in tiles]
    cs = [jnp.zeros((SB_BK, 1), F32) for _ in tiles]
    accs = [jnp.zeros((SB_BK, SB_HEAD), F32) for _ in tiles]
    base = qi * SB_BQ
    cs, accs = blocks(qs, [base + h * SB_BK for h in tiles], 1, cs, accs, True)
    c1, a1 = blocks(qs[1:], [base], 1, cs[1:], accs[1:], False)
    cs, accs = [cs[0]] + c1, [accs[0]] + a1

    def run(nsub, start):
        def body(n, carry):
            k0 = start - (n + 1) * nsub * SB_BK
            c, a = blocks(qs, [k0, k0], nsub, list(carry[0]), list(carry[1]), False)
            return tuple(c), tuple(a)
        return body

    pairs_per_group = SB_GROUP // 2
    rem = qi % pairs_per_group
    carry = lax.fori_loop(0, rem, run(2, base), (tuple(cs), tuple(accs)))
    cs, accs = lax.fori_loop(0, qi // pairs_per_group, run(SB_GROUP, base - rem * 2 * SB_BK), carry)
    for h in tiles:
        o_ref[h * SB_BK:(h + 1) * SB_BK, :] = accs[h].astype(o_ref.dtype)


def _stick_breaking(p, qg, kg, bsz, seq):
    assert SB_BQ == 2 * SB_BK
    nq = seq // SB_BQ
    qb, kb, vb = COL_SBQ // SB_HEAD, COL_SBK // SB_HEAD, COL_SBV // SB_HEAD
    return pl.pallas_call(
        functools.partial(_sb_kernel, seq=seq),
        grid=(bsz, SB_HEADS, nq),
        in_specs=[
            pl.BlockSpec((SB_BQ, SB_HEAD), lambda b, h, q: (b * nq + q, qb + h)),
            pl.BlockSpec((seq, SB_HEAD), lambda b, h, q: (b, kb + h)),
            pl.BlockSpec((seq, SB_HEAD), lambda b, h, q: (b, vb + h)),
            pl.BlockSpec((1, SB_HEAD), lambda b, h, q: (0, 0)),
            pl.BlockSpec((1, SB_HEAD), lambda b, h, q: (0, 0)),
        ],
        out_specs=pl.BlockSpec((SB_BQ, SB_HEAD), lambda b, h, q: (b * nq + q, h)),
        out_shape=jax.ShapeDtypeStruct((bsz * seq, SB_W), BF16),
        scratch_shapes=[pltpu.VMEM((seq, SB_HEAD), BF16), pltpu.VMEM((seq, SB_HEAD), BF16)],
        compiler_params=_cparams(("parallel", "parallel", "arbitrary")),
        name="stick_breaking",
    )(p, p, p, qg.reshape(1, -1), kg.reshape(1, -1))


PREP_HALO = 8


def _head_ones():
    r = lax.broadcasted_iota(jnp.int32, (PAIR, PAIR), 0) // RWKV_HEAD
    c = lax.broadcasted_iota(jnp.int32, (PAIR, PAIR), 1) // RWKV_HEAD
    return (r == c).astype(BF16)


def _rwkv_prep_kernel(p_ref, lora_ref, hp_ref, hlora_ref, mu_ref, mul_ref, w0_ref, wup_ref, a0_ref,
                      aup_ref, gup_ref, kk_ref, ka_ref, rk_ref,
                      r_o, k_o, v_o, lw_o, a_o, b_o, bonus_o, g_o, *, first):
    def shifted(x_ref, h_ref, m_ref, c0, c1):
        x = x_ref[:, c0:c1]
        prev = pltpu.roll(x, 1, 0)
        row = lax.broadcasted_iota(jnp.int32, x.shape, 0)
        last = h_ref[PREP_HALO - 1:PREP_HALO, c0:c1]
        last = jnp.where(first, jnp.zeros_like(last), last)
        prev = jnp.where(row == 0, last, prev)
        return x + m_ref[:, c0:c1] * (prev - x)

    lo = shifted(lora_ref, hlora_ref, mul_ref, 0, LORA_COLS)
    w_lo = lo[:, 0:LORA_PAD]
    a_lo = lo[:, LORA_PAD:2 * LORA_PAD]
    g_lo = lo[:, 2 * LORA_PAD:]
    ones = _head_ones()

    for c0 in range(0, RWKV_W, 512):
        c1 = c0 + 512
        r = shifted(p_ref, hp_ref, mu_ref, c0, c1)
        k = shifted(p_ref, hp_ref, mu_ref, RWKV_W + c0, RWKV_W + c1)
        v = shifted(p_ref, hp_ref, mu_ref, 2 * RWKV_W + c0, 2 * RWKV_W + c1)
        y = w0_ref[:, c0:c1] + _dot(jnp.tanh(w_lo), wup_ref[:, c0:c1])
        log_w = -_softplus(-y) - 0.5
        lw_o[:, c0:c1] = -jnp.exp(log_w)
        iclr = jax.nn.sigmoid(a0_ref[:, c0:c1] + _dot(a_lo, aup_ref[:, c0:c1]))
        g_o[:, c0:c1] = _dot(jax.nn.sigmoid(g_lo), gup_ref[:, c0:c1])
        kk = k * kk_ref[:, c0:c1]
        kt = k * (1.0 + (iclr - 1.0) * ka_ref[:, c0:c1])
        rkr = r * kt * rk_ref[:, c0:c1]
        for t0 in range(0, 512, PAIR):
            t1 = t0 + PAIR
            kk_t = kk[:, t0:t1]
            ss = _dot_x2(kk_t * kk_t, ones)
            kk_n = kk_t * lax.rsqrt(jnp.maximum(ss, 1e-24))
            a_o[:, c0 + t0:c0 + t1] = -kk_n
            b_o[:, c0 + t0:c0 + t1] = kk_n * iclr[:, t0:t1]
            bonus_o[:, c0 + t0:c0 + t1] = _dot(rkr[:, t0:t1], ones) * v[:, t0:t1]
        r_o[:, c0:c1] = r
        k_o[:, c0:c1] = kt
        v_o[:, c0:c1] = v


def _rwkv_prep(p, mu, mul, w0, wup, a0, aup, gup, kk, ka, rk, seq, tm=128):
    m = p.shape[0]
    lb = COL_LORA // LORA_COLS
    hmap = lambda i: jnp.maximum(i * (tm // PREP_HALO) - 1, 0)
    row = lambda a: a.reshape(1, -1)
    vec = pl.BlockSpec((1, RWKV_W), lambda i: (0, 0))
    out = jax.ShapeDtypeStruct((m, RWKV_W), F32)
    ospec = pl.BlockSpec((tm, RWKV_W), lambda i: (i, 0))
    return pl.pallas_call(
        functools.partial(_rwkv_prep_kernel, tm=tm, seq=seq),
        grid=(m // tm,),
        in_specs=[
            pl.BlockSpec((tm, 3 * RWKV_W), lambda i: (i, 0)),
            pl.BlockSpec((tm, LORA_COLS), lambda i: (i, lb)),
            pl.BlockSpec((PREP_HALO, 3 * RWKV_W), lambda i: (hmap(i), 0)),
            pl.BlockSpec((PREP_HALO, LORA_COLS), lambda i: (hmap(i), lb)),
            pl.BlockSpec((1, 3 * RWKV_W), lambda i: (0, 0)),
            pl.BlockSpec((1, LORA_COLS), lambda i: (0, 0)),
            vec,
            pl.BlockSpec((LORA_PAD, RWKV_W), lambda i: (0, 0)),
            vec,
            pl.BlockSpec((LORA_PAD, RWKV_W), lambda i: (0, 0)),
            pl.BlockSpec((GATE_LORA, RWKV_W), lambda i: (0, 0)),
            vec, vec, vec,
        ],
        out_specs=[ospec] * 8,
        out_shape=[out] * 8,
        compiler_params=_cparams(("parallel",)),
        name="rwkv_prep",
    )(p, p, p, p, row(mu), row(mul), row(w0), wup, row(a0), aup, gup, row(kk), row(ka), row(rk))


SCAN_PAIRS = 2
GN_ROWS = 256


def _rwkv_scan_kernel(r_ref, k_ref, v_ref, lw_ref, a_ref, b_ref, bonus_ref, g_ref, gng_ref, gnb_ref,
                      o_ref, s_ref, rq_s, oin_s, tr_s, dl_s, *, tblk):
    C = RWKV_C
    t_id = pl.program_id(2)

    @pl.when(t_id == 0)
    def _():
        s_ref[...] = jnp.zeros_like(s_ref)

    lane = lax.broadcasted_iota(jnp.int32, (1, PAIR), 1)
    m0 = (lane < RWKV_HEAD).astype(F32)
    m1 = 1.0 - m0
    ti = lax.broadcasted_iota(jnp.int32, (C, C), 0)
    ii = lax.broadcasted_iota(jnp.int32, (C, C), 1)
    tri_incl = (ii <= ti).astype(BF16)
    t2 = lax.broadcasted_iota(jnp.int32, (C, 2 * C), 0)
    i2 = lax.broadcasted_iota(jnp.int32, (C, 2 * C), 1) % C
    strict2 = (i2 < t2).astype(F32)
    incl2 = (i2 <= t2).astype(F32)
    rr = lax.broadcasted_iota(jnp.int32, (PAIR, PAIR), 0)
    cc = lax.broadcasted_iota(jnp.int32, (PAIR, PAIR), 1)
    bdmask = (rr // RWKV_HEAD == cc // RWKV_HEAD).astype(F32)
    eye = (rr == cc).astype(F32)
    ones = _head_ones()

    def stack_heads(x):
        return jnp.concatenate([x * m0, x * m1], axis=0)

    def chunk_pair(c, pi):
        sl = pl.ds(pl.multiple_of(c * C, C), C)
        ln = slice(pi * PAIR, (pi + 1) * PAIR)
        lw = lw_ref[sl, ln]
        r, k, v, a, b = r_ref[sl, ln], k_ref[sl, ln], v_ref[sl, ln], a_ref[sl, ln], b_ref[sl, ln]
        cum = _x3_dot(tri_incl, lw)
        tot = cum[C - 1:C, :]
        e_neg = jnp.exp(-cum)
        e_end = jnp.exp(tot - cum)
        at = a * jnp.exp(cum - lw)
        rt = r * jnp.exp(cum)
        bh, kh = b * e_neg, k * e_neg
        bend, kend = b * e_end, k * e_end
        gram = _dot_nt(jnp.concatenate([at, rt], axis=0),
                       jnp.concatenate([stack_heads(bh), stack_heads(kh)], axis=0))
        m_ab = gram[0:C, 0:2 * C] * strict2
        m_ak = gram[0:C, 2 * C:] * strict2
        q_b = gram[C:, 0:2 * C] * incl2
        q_k = gram[C:, 2 * C:] * incl2
        vv = stack_heads(v)
        u_rhs = _dot(m_ak, vv)
        bd = stack_heads(m_ab)
        t_inv = eye + bd
        pw = bd
        for _ in range(5):
            pw = _dot(pw, pw)
            t_inv = t_inv + _dot(t_inv, pw)
        t_ls = t_inv[0:C, :] + t_inv[C:, :]
        sol = _dot(t_ls, jnp.concatenate(
            [jnp.concatenate([at * m0, u_rhs * m0], axis=1),
             jnp.concatenate([at * m1, u_rhs * m1], axis=1)], axis=0))
        wa, u0 = sol[:, 0:PAIR], sol[:, PAIR:]
        y = _dot(q_b, jnp.concatenate(
            [jnp.concatenate([wa * m0, u0 * m0], axis=1),
             jnp.concatenate([wa * m1, u0 * m1], axis=1)], axis=0))
        rq_s[sl, ln] = rt + y[:, 0:PAIR]
        oin_s[sl, ln] = y[:, PAIR:] + _dot(q_k, vv)
        tr_s[pi, c] = _dot_tn(wa, bend) * bdmask + eye * jnp.exp(tot)
        dl_s[pi, c] = (_dot_tn(u0, bend) + _dot_tn(v, kend)) * bdmask

    def chunk(c, carry):
        for pi in range(SCAN_PAIRS):
            chunk_pair(c, pi)
        return carry

    lax.fori_loop(0, tblk // C, chunk, 0)

    def scan(c, carry):
        sl = pl.ds(pl.multiple_of(c * C, C), C)
        for pi in range(SCAN_PAIRS):
            ln = slice(pi * PAIR, (pi + 1) * PAIR)
            s = s_ref[pi]
            oin_s[sl, ln] = _dot_nt(rq_s[sl, ln], s) + oin_s[sl, ln]
            s_ref[pi] = _dot(s, tr_s[pi, c]) + dl_s[pi, c]
        return carry

    lax.fori_loop(0, tblk // C, scan, 0)

    for pi in range(SCAN_PAIRS):
        ln = slice(pi * PAIR, (pi + 1) * PAIR)
        for r0 in range(0, tblk, GN_ROWS):
            rows = slice(r0, r0 + GN_ROWS)
            o = oin_s[rows, ln]
            mean = _dot_x3(o, ones) * (1.0 / RWKV_HEAD)
            d = o - mean
            var = _dot_x3(d * d, ones) * (1.0 / RWKV_HEAD)
            on = d * lax.rsqrt(var + GN_EPS) * gng_ref[:, ln] + gnb_ref[:, ln]
            o_ref[rows, ln] = ((on + bonus_ref[rows, ln]) * g_ref[rows, ln]).astype(o_ref.dtype)


def _rwkv_scan(r, k, v, lw, a, b, bonus, g, gng, gnb, bsz, seq, tblk=512):
    nt = seq // tblk
    width = SCAN_PAIRS * PAIR
    tile = pl.BlockSpec((tblk, width), lambda bi, h, t: (bi * nt + t, h))
    vec = pl.BlockSpec((1, width), lambda bi, h, t: (0, h))
    nch = tblk // RWKV_C
    return pl.pallas_call(
        functools.partial(_rwkv_scan_kernel, tblk=tblk),
        grid=(bsz, RWKV_W // width, nt),
        in_specs=[tile] * 8 + [vec, vec],
        out_specs=tile,
        out_shape=jax.ShapeDtypeStruct((bsz * seq, RWKV_W), BF16),
        scratch_shapes=[pltpu.VMEM((SCAN_PAIRS, PAIR, PAIR), F32),
                        pltpu.VMEM((tblk, width), F32), pltpu.VMEM((tblk, width), F32),
                        pltpu.VMEM((SCAN_PAIRS, nch, PAIR, PAIR), F32),
                        pltpu.VMEM((SCAN_PAIRS, nch, PAIR, PAIR), F32)],
        compiler_params=_cparams(("parallel", "parallel", "arbitrary")),
        name="rwkv_scan",
    )(r, k, v, lw, a, b, bonus, g, gng.reshape(1, -1), gnb.reshape(1, -1))


QUAD_HEADS = 2
QUAD = QUAD_HEADS * RWKV_HEAD
SCAN_GROUPS = 16
SCAN_ROWS = 256


def _rwkv_quad_kernel(r_ref, k_ref, v_ref, lw_ref, a_ref, b_ref, bonus_ref, g_ref, gng_ref, gnb_ref,
                      o_ref, s_ref, rq_s, oin_s, tr_s, dl_s, *, tblk, t_id):
    C = RWKV_C
    HC = QUAD_HEADS * C
    G = range(SCAN_GROUPS)

    @pl.when(t_id == 0)
    def _():
        s_ref[...] = jnp.zeros_like(s_ref)

    lane = lax.broadcasted_iota(jnp.int32, (1, QUAD), 1) // RWKV_HEAD
    hm = [(lane == h).astype(F32) for h in range(QUAD_HEADS)]
    ti = lax.broadcasted_iota(jnp.int32, (C, C), 0)
    ii = lax.broadcasted_iota(jnp.int32, (C, C), 1)
    tri_incl = (ii <= ti).astype(BF16)
    t4 = lax.broadcasted_iota(jnp.int32, (C, HC), 0)
    i4 = lax.broadcasted_iota(jnp.int32, (C, HC), 1) % C
    strict4 = (i4 < t4).astype(F32)
    incl4 = (i4 <= t4).astype(F32)
    rr = lax.broadcasted_iota(jnp.int32, (QUAD, QUAD), 0)
    cc = lax.broadcasted_iota(jnp.int32, (QUAD, QUAD), 1)
    bdmask = (rr // RWKV_HEAD == cc // RWKV_HEAD).astype(F32)
    eye = (rr == cc).astype(F32)
    ones = (rr // RWKV_HEAD == cc // RWKV_HEAD).astype(BF16)

    def stack_heads(x):
        return jnp.concatenate([x * m for m in hm], axis=0)

    def stack_heads2(x, y):
        return jnp.concatenate([jnp.concatenate([x * m, y * m], axis=1) for m in hm], axis=0)

    def chunk(c, carry):
        sl = pl.ds(pl.multiple_of(c * C, C), C)
        lns = [slice(g * QUAD, (g + 1) * QUAD) for g in G]
        lw = [lw_ref[sl, ln] for ln in lns]
        cum = [_x3_dot(tri_incl, x) for x in lw]
        tot = [x[C - 1:C, :] for x in cum]
        e_neg = [jnp.exp(-x) for x in cum]
        e_end = [jnp.exp(t - x) for t, x in zip(tot, cum)]
        at = [a_ref[sl, ln] * jnp.exp(x - w) for ln, x, w in zip(lns, cum, lw)]
        rt = [r_ref[sl, ln] * jnp.exp(x) for ln, x in zip(lns, cum)]
        b = [b_ref[sl, ln] for ln in lns]
        k = [k_ref[sl, ln] for ln in lns]
        v = [v_ref[sl, ln] for ln in lns]
        gram = [_dot_nt(jnp.concatenate([at[g], rt[g]], axis=0),
                        jnp.concatenate([stack_heads(b[g] * e_neg[g]), stack_heads(k[g] * e_neg[g])], axis=0))
                for g in G]
        m_ak = [x[0:C, HC:] * strict4 for x in gram]
        q_b = [x[C:, 0:HC] * incl4 for x in gram]
        q_k = [x[C:, HC:] * incl4 for x in gram]
        vv = [stack_heads(x) for x in v]
        u_rhs = [_dot(m_ak[g], vv[g]) for g in G]
        pw = [stack_heads(x[0:C, 0:HC] * strict4) for x in gram]
        t_inv = [eye + x for x in pw]
        pw = [_dot(x, x) for x in pw]
        for _ in range(4):
            both = [_dot(jnp.concatenate([x, t], axis=0), x) for t, x in zip(t_inv, pw)]
            pw = [x[0:HC, :] for x in both]
            t_inv = [t + x[HC:, :] for t, x in zip(t_inv, both)]
        t_inv = [t + _dot(t, x) for t, x in zip(t_inv, pw)]
        t_ls = [sum(t[h * C:(h + 1) * C, :] for h in range(1, QUAD_HEADS)) + t[0:C, :] for t in t_inv]
        sol = [_dot(t_ls[g], stack_heads2(at[g], u_rhs[g])) for g in G]
        wa = [x[:, 0:QUAD] for x in sol]
        u0 = [x[:, QUAD:] for x in sol]
        y = [_dot(q_b[g], stack_heads2(wa[g], u0[g])) for g in G]
        o_in2 = [_dot(q_k[g], vv[g]) for g in G]
        bend = [b[g] * e_end[g] for g in G]
        tr = [_dot_tn(wa[g], bend[g]) for g in G]
        dl = [_dot_tn(jnp.concatenate([u0[g], v[g]], axis=0),
                      jnp.concatenate([bend[g], k[g] * e_end[g]], axis=0)) for g in G]
        for g in G:
            rq_s[sl, lns[g]] = rt[g] + y[g][:, 0:QUAD]
            oin_s[sl, lns[g]] = y[g][:, QUAD:] + o_in2[g]
            tr_s[g, c] = tr[g] * bdmask + eye * jnp.exp(tot[g])
            dl_s[g, c] = dl[g] * bdmask
        return carry

    lax.fori_loop(0, tblk // C, chunk, 0)

    def scan(c, carry):
        sl = pl.ds(pl.multiple_of(c * C, C), C)
        for g in G:
            ln = slice(g * QUAD, (g + 1) * QUAD)
            s = s_ref[g]
            oin_s[sl, ln] = _dot_nt(rq_s[sl, ln], s) + oin_s[sl, ln]
            s_ref[g] = _dot(s, tr_s[g, c]) + dl_s[g, c]
        return carry

    lax.fori_loop(0, tblk // C, scan, 0)

    for g in G:
        ln = slice(g * QUAD, (g + 1) * QUAD)
        for r0 in range(0, tblk, GN_ROWS):
            rows = slice(r0, r0 + GN_ROWS)
            o = oin_s[rows, ln]
            mean = _dot_x3(o, ones) * (1.0 / RWKV_HEAD)
            d = o - mean
            var = _dot_x2(d * d, ones) * (1.0 / RWKV_HEAD)
            on = d * lax.rsqrt(var + GN_EPS) * gng_ref[:, ln] + gnb_ref[:, ln]
            o_ref[rows, ln] = ((on + bonus_ref[rows, ln]) * g_ref[rows, ln]).astype(o_ref.dtype)


def _rwkv_quad_scan(r, k, v, lw, a, b, bonus, g, gng, gnb, bsz, seq, tblk=512):
    nt = seq // tblk
    width = SCAN_GROUPS * QUAD
    tile = pl.BlockSpec((tblk, width), lambda bi, h, t: (bi * nt + t, h))
    vec = pl.BlockSpec((1, width), lambda bi, h, t: (0, h))
    nch = tblk // RWKV_C
    return pl.pallas_call(
        functools.partial(_rwkv_quad_kernel, tblk=tblk),
        grid=(bsz, RWKV_W // width, nt),
        in_specs=[tile] * 8 + [vec, vec],
        out_specs=tile,
        out_shape=jax.ShapeDtypeStruct((bsz * seq, RWKV_W), BF16),
        scratch_shapes=[pltpu.VMEM((SCAN_GROUPS, QUAD, QUAD), F32),
                        pltpu.VMEM((tblk, width), F32), pltpu.VMEM((tblk, width), F32),
                        pltpu.VMEM((SCAN_GROUPS, nch, QUAD, QUAD), F32),
                        pltpu.VMEM((SCAN_GROUPS, nch, QUAD, QUAD), F32)],
        compiler_params=_cparams(("parallel", "parallel", "arbitrary")),
        name="rwkv_scan",
    )(r, k, v, lw, a, b, bonus, g, gng.reshape(1, -1), gnb.reshape(1, -1))


def _rwkv_fused_kernel(p_ref, lora_ref, hp_ref, hlora_ref, mu_ref, mul_ref, w0_ref, wup_ref, a0_ref,
                       aup_ref, gup_ref, kk_ref, ka_ref, rk_ref, gng_ref, gnb_ref, o_ref,
                       r_s, k_s, v_s, lw_s, a_s, b_s, bonus_s, g_s, s_ref, rq_s, oin_s, tr_s, dl_s, *, tblk):
    t_id = pl.program_id(1)
    _rwkv_prep_kernel(p_ref, lora_ref, hp_ref, hlora_ref, mu_ref, mul_ref, w0_ref, wup_ref, a0_ref,
                      aup_ref, gup_ref, kk_ref, ka_ref, rk_ref,
                      r_s, k_s, v_s, lw_s, a_s, b_s, bonus_s, g_s, first=t_id == 0)
    _rwkv_quad_kernel(r_s, k_s, v_s, lw_s, a_s, b_s, bonus_s, g_s, gng_ref, gnb_ref,
                      o_ref, s_ref, rq_s, oin_s, tr_s, dl_s, tblk=tblk, t_id=t_id)


def _rwkv_mix(p, mu, mul, w0, wup, a0, aup, gup, kk, ka, rk, gng, gnb, bsz, seq, tblk):
    assert SCAN_GROUPS * QUAD == RWKV_W and RWKV_C == RWKV_HEAD
    nt = seq // tblk
    nch = tblk // RWKV_C
    lb = COL_LORA // LORA_COLS
    blk = lambda b, t: b * nt + t
    hmap = lambda b, t: jnp.maximum(blk(b, t) * (tblk // PREP_HALO) - 1, 0)
    row = lambda a: a.reshape(1, -1)
    vec = pl.BlockSpec((1, RWKV_W), lambda b, t: (0, 0))
    full = lambda r, c: pl.BlockSpec((r, c), lambda b, t: (0, 0))
    act = pltpu.VMEM((tblk, RWKV_W), F32)
    return pl.pallas_call(
        functools.partial(_rwkv_fused_kernel, tblk=tblk),
        grid=(bsz, nt),
        in_specs=[
            pl.BlockSpec((tblk, 3 * RWKV_W), lambda b, t: (blk(b, t), 0)),
            pl.BlockSpec((tblk, LORA_COLS), lambda b, t: (blk(b, t), lb)),
            pl.BlockSpec((PREP_HALO, 3 * RWKV_W), lambda b, t: (hmap(b, t), 0)),
            pl.BlockSpec((PREP_HALO, LORA_COLS), lambda b, t: (hmap(b, t), lb)),
            full(1, 3 * RWKV_W), full(1, LORA_COLS),
            vec, full(LORA_PAD, RWKV_W), vec, full(LORA_PAD, RWKV_W), full(GATE_LORA, RWKV_W),
            vec, vec, vec, vec, vec,
        ],
        out_specs=pl.BlockSpec((tblk, RWKV_W), lambda b, t: (blk(b, t), 0)),
        out_shape=jax.ShapeDtypeStruct((bsz * seq, RWKV_W), BF16),
        scratch_shapes=[act] * 8 + [
            pltpu.VMEM((SCAN_GROUPS, QUAD, QUAD), F32), act, act,
            pltpu.VMEM((SCAN_GROUPS, nch, QUAD, QUAD), F32),
            pltpu.VMEM((SCAN_GROUPS, nch, QUAD, QUAD), F32)],
        compiler_params=_cparams(("parallel", "arbitrary")),
        name="rwkv_mix",
    )(p, p, p, p, row(mu), row(mul), row(w0), wup, row(a0), aup, gup, row(kk), row(ka), row(rk),
      row(gng), row(gnb))


def _relayout_cols(w, axis):
    def sl(a, b):
        idx = [slice(None)] * w.ndim
        idx[axis] = slice(a, b)
        return w[tuple(idx)]
    pad_shape = list(w.shape)
    pad_shape[axis] = LORA_PAD - DECAY_LORA
    z = jnp.zeros(pad_shape, w.dtype)
    lo = 3 * RWKV_W
    rest = lo + DECAY_LORA + ICLR_LORA + GATE_LORA
    return jnp.concatenate(
        [sl(0, lo), sl(rest, w.shape[axis]), sl(lo, lo + DECAY_LORA), z,
         sl(lo + DECAY_LORA, lo + DECAY_LORA + ICLR_LORA), z,
         sl(lo + DECAY_LORA + ICLR_LORA, rest)], axis=axis)


def _pad_rows(w):
    return jnp.concatenate([w, jnp.zeros((LORA_PAD - w.shape[0], w.shape[1]), w.dtype)], axis=0)


def _layer(x, bsz, seq, layer, w_in, w_out, attn_norm, ffn_norm, mu, w0, w_up, a0, a_up, g_up, k_k, k_a, r_k,
           gn_g, gn_b, conv_w, conv_b, ln_g, ln_b, q_norm, k_norm, ffn_up, ffn_conv, ffn_down, tm_mm=1024):
    h = _rmsnorm(x, attn_norm)
    p = _matmul_nt(h, w_in, layer, tm=tm_mm, name="proj_in")
    mu_p, mu_l = mu[:3 * RWKV_W], mu[COL_LORA:]
    y_a = _rwkv_mix(p, mu_p, mu_l, w0, w_up, a0, a_up, g_up, k_k, k_a, r_k.reshape(-1), gn_g, gn_b,
                    bsz, seq, tblk=min(SCAN_ROWS, seq))
    y_b = _conformer(p, conv_w, conv_b, ln_g, ln_b, seq)
    y_c = _stick_breaking(p, q_norm, k_norm, bsz, seq)
    x = _matmul([y_a, y_b, y_c], w_out, layer, residual=x, tm=tm_mm, name="proj_out")
    h = _rmsnorm(x, ffn_norm)
    act = _ffn_up(h, ffn_up, layer, ffn_conv, seq, tm=tm_mm)
    return _matmul([act], ffn_down, layer, residual=x, tm=tm_mm, tn=256, name="ffn_down")


def kernel(x, w_in, w_out, attn_norm, ffn_norm, rwkv_mu, rwkv_w0, rwkv_w_up, rwkv_a0, rwkv_a_up,
           rwkv_g_up, rwkv_k_k, rwkv_k_a, rwkv_r_k, rwkv_gn_g, rwkv_gn_b, conv_w, conv_b, conv_ln_g,
           conv_ln_b, sb_q_norm, sb_k_norm, ffn_up, ffn_conv, ffn_down):
    bsz, seq, d = x.shape
    x = x.reshape(bsz * seq, d)
    w_in_b = _relayout_cols(jnp.swapaxes(w_in, 1, 2).astype(BF16), 1)
    w_out_b, ffn_up_b, ffn_down_b = w_out.astype(BF16), ffn_up.astype(BF16), ffn_down.astype(BF16)
    for l in range(w_in.shape[0]):
        mu_full = jnp.concatenate([rwkv_mu[l], jnp.zeros((w_in.shape[2] - rwkv_mu.shape[1],), F32)])
        mu_re = _relayout_cols(mu_full, 0)
        x = _layer(
            x, bsz, seq, l, w_in_b, w_out_b, attn_norm[l], ffn_norm[l],
            mu_re, rwkv_w0[l], _pad_rows(rwkv_w_up[l]).astype(BF16), rwkv_a0[l],
            _pad_rows(rwkv_a_up[l]).astype(BF16), rwkv_g_up[l].astype(BF16), rwkv_k_k[l], rwkv_k_a[l],
            rwkv_r_k[l], rwkv_gn_g[l], rwkv_gn_b[l], conv_w[l], conv_b[l], conv_ln_g[l], conv_ln_b[l],
            sb_q_norm[l], sb_k_norm[l], ffn_up_b, ffn_conv[l], ffn_down_b,
            tm_mm=min(1024, seq))
    return x.reshape(bsz, seq, d)
```

```python
import functools

import jax
import jax.numpy as jnp
from jax import lax
from jax.experimental import pallas as pl
from jax.experimental.pallas import tpu as pltpu

F32 = jnp.float32
BF16 = jnp.bfloat16

D_MODEL = 4096
RWKV_HEAD = 64
RWKV_W = 2048
DECAY_LORA = 96
ICLR_LORA = 96
GATE_LORA = 256
CONV_W = 1024
CONV_K = 31
SB_HEAD = 128
SB_W = 1024
SB_HEADS = SB_W // SB_HEAD
D_FF = 2 * D_MODEL
FFN_CONV_K = 3
NORM_EPS = 1e-6
LN_EPS = 1e-5
GN_EPS = RWKV_HEAD * 1e-5

SUBLANES = 8
LORA_PAD = 128
COL_CVAL = 3 * RWKV_W
COL_CGATE = COL_CVAL + CONV_W
COL_SBQ = COL_CGATE + CONV_W
COL_SBK = COL_SBQ + SB_W
COL_SBV = COL_SBK + SB_W
COL_LORA = COL_SBV + SB_W
LORA_COLS = 2 * LORA_PAD + GATE_LORA
PROJ_PAD = COL_LORA + LORA_COLS
VMEM_LIMIT = 56 * 1024 * 1024

RWKV_C = 64
PAIR = 2 * RWKV_HEAD
PREP_COLS = 512


def _cparams(sem):
    return pltpu.CompilerParams(dimension_semantics=sem, vmem_limit_bytes=VMEM_LIMIT)


def _dot(a, b):
    return jnp.dot(a.astype(BF16), b.astype(BF16), preferred_element_type=F32)


def _dot_nt(a, b):
    return lax.dot_general(a.astype(BF16), b.astype(BF16), (((1,), (1,)), ((), ())),
                           preferred_element_type=F32)


def _dot_tn(a, b):
    return lax.dot_general(a.astype(BF16), b.astype(BF16), (((0,), (0,)), ((), ())),
                           preferred_element_type=F32)


def _split3(x):
    hi = x.astype(BF16)
    r1 = x - hi.astype(F32)
    mid = r1.astype(BF16)
    lo = (r1 - mid.astype(F32)).astype(BF16)
    return hi, mid, lo


def _dot_x3(a, b_exact):
    hi, mid, lo = _split3(a)
    b = b_exact.astype(BF16)
    return (jnp.dot(hi, b, preferred_element_type=F32)
            + jnp.dot(mid, b, preferred_element_type=F32)
            + jnp.dot(lo, b, preferred_element_type=F32))


def _x3_dot(a_exact, b):
    hi, mid, lo = _split3(b)
    a = a_exact.astype(BF16)
    return (jnp.dot(a, hi, preferred_element_type=F32)
            + jnp.dot(a, mid, preferred_element_type=F32)
            + jnp.dot(a, lo, preferred_element_type=F32))


def _dot_x2(a, b_exact):
    hi = a.astype(BF16)
    lo = (a - hi.astype(F32)).astype(BF16)
    b = b_exact.astype(BF16)
    return jnp.dot(hi, b, preferred_element_type=F32) + jnp.dot(lo, b, preferred_element_type=F32)


def _neg_abs(x):
    bits = lax.bitcast_convert_type(x, jnp.int32) | jnp.int32(-2 ** 31)
    return lax.bitcast_convert_type(bits, F32)


def _softplus(x):
    return jnp.maximum(x, 0.0) + jnp.log(1.0 + jnp.exp(_neg_abs(x)))


def _rmsnorm_kernel(x_ref, g_ref, o_ref):
    x = x_ref[...]
    ms = jnp.mean(x * x, axis=-1, keepdims=True)
    o_ref[...] = (x * lax.rsqrt(ms + NORM_EPS) * g_ref[...]).astype(o_ref.dtype)


def _rmsnorm(x, g, tm=512):
    m, d = x.shape
    return pl.pallas_call(
        _rmsnorm_kernel,
        grid=(m // tm,),
        in_specs=[pl.BlockSpec((tm, d), lambda i: (i, 0)),
                  pl.BlockSpec((1, d), lambda i: (0, 0))],
        out_specs=pl.BlockSpec((tm, d), lambda i: (i, 0)),
        out_shape=jax.ShapeDtypeStruct((m, d), BF16),
        compiler_params=_cparams(("parallel",)),
        name="rmsnorm",
    )(x, g.reshape(1, d))


def _mm_kernel(*refs, n_lhs, has_res):
    o_ref = refs[-1]
    acc = None
    for a_ref, w_ref in zip(refs[:n_lhs], refs[n_lhs:2 * n_lhs]):
        d = jnp.dot(a_ref[...], w_ref[...], preferred_element_type=F32)
        acc = d if acc is None else acc + d
    if has_res:
        acc = acc + refs[2 * n_lhs][...]
    o_ref[...] = acc.astype(o_ref.dtype)


def _mm_nt_kernel(a_ref, w_ref, o_ref):
    o_ref[...] = lax.dot_general(a_ref[...], w_ref[...], (((1,), (1,)), ((), ())),
                                 preferred_element_type=F32)


def _matmul_nt(a, w_t, layer, tm=1024, tn=512, name="matmul_nt"):
    m, k = a.shape
    n = w_t.shape[1]
    return pl.pallas_call(
        _mm_nt_kernel,
        grid=(m // tm, n // tn),
        in_specs=[pl.BlockSpec((tm, k), lambda i, j: (i, 0)),
                  pl.BlockSpec((pl.Squeezed(), tn, k), lambda i, j: (layer, j, 0))],
        out_specs=pl.BlockSpec((tm, tn), lambda i, j: (i, j)),
        out_shape=jax.ShapeDtypeStruct((m, n), F32),
        compiler_params=_cparams(("parallel", "arbitrary")),
        name=name,
    )(a, w_t)


def _matmul(lhs_list, w, layer, residual=None, tm=1024, tn=512, name="matmul"):
    m = lhs_list[0].shape[0]
    n = w.shape[2]
    in_specs, args = [], []
    for a in lhs_list:
        in_specs.append(pl.BlockSpec((tm, a.shape[1]), lambda i, j: (i, 0)))
        args.append(a)
    row = 0
    for a in lhs_list:
        k = a.shape[1]
        assert row % k == 0
        in_specs.append(pl.BlockSpec((pl.Squeezed(), k, tn),
                                     functools.partial(lambda i, j, rb: (layer, rb, j), rb=row // k)))
        args.append(w)
        row += k
    assert row == w.shape[1]
    if residual is not None:
        in_specs.append(pl.BlockSpec((tm, tn), lambda i, j: (i, j)))
        args.append(residual)
    return pl.pallas_call(
        functools.partial(_mm_kernel, n_lhs=len(lhs_list), has_res=residual is not None),
        grid=(m // tm, n // tn),
        in_specs=in_specs,
        out_specs=pl.BlockSpec((tm, tn), lambda i, j: (i, j)),
        out_shape=jax.ShapeDtypeStruct((m, n), F32),
        compiler_params=_cparams(("parallel", "arbitrary")),
        name=name,
    )(*args)


FFN_CARRY = 8
FFN_SUB = 256


def _ffn_up_kernel(h_ref, wg_ref, wv_ref, cg_ref, cv_ref, o_ref, *scratch, tm, tn, seq):
    i = pl.program_id(1)
    first = (i * tm) % seq == 0
    nsub = tn // FFN_SUB
    sg, sv = scratch[:nsub], scratch[nsub:]

    @pl.when(first)
    def _():
        for s in scratch:
            s[0:FFN_CARRY, :] = jnp.zeros((FFN_CARRY, FFN_SUB), F32)

    @pl.when(jnp.logical_not(first))
    def _():
        for s in scratch:
            s[0:FFN_CARRY, :] = s[tm:tm + FFN_CARRY, :]

    def up(w_ref, s, n):
        s[FFN_CARRY:FFN_CARRY + tm, :] = jnp.dot(h_ref[...], w_ref[:, n * FFN_SUB:(n + 1) * FFN_SUB],
                                                 preferred_element_type=F32)

    def conv(c_ref, s, n):
        acc = None
        for k in range(FFN_CONV_K):
            off = FFN_CARRY - (FFN_CONV_K - 1) + k
            term = c_ref[k:k + 1, n * FFN_SUB:(n + 1) * FFN_SUB] * s[off:off + tm, :]
            acc = term if acc is None else acc + term
        return acc

    def gate(n):
        g = conv(cg_ref, sg[n], n)
        v = conv(cv_ref, sv[n], n)
        silu = 0.5 * g * (1.0 + jnp.tanh(0.5 * g))
        o_ref[:, n * FFN_SUB:(n + 1) * FFN_SUB] = (silu * v).astype(o_ref.dtype)

    up(wg_ref, sg[0], 0)
    up(wv_ref, sv[0], 0)
    for n in range(1, nsub):
        up(wg_ref, sg[n], n)
        gate(n - 1)
        up(wv_ref, sv[n], n)
    gate(nsub - 1)


def _ffn_up(h, w_up, layer, conv, seq, tm=1024, tn=512):
    m, d = h.shape
    nj = D_FF // tn
    return pl.pallas_call(
        functools.partial(_ffn_up_kernel, tm=tm, tn=tn, seq=seq),
        grid=(nj, m // tm),
        in_specs=[
            pl.BlockSpec((tm, d), lambda j, i: (i, 0)),
            pl.BlockSpec((pl.Squeezed(), d, tn), lambda j, i: (layer, 0, j)),
            pl.BlockSpec((pl.Squeezed(), d, tn), lambda j, i: (layer, 0, j + nj)),
            pl.BlockSpec((FFN_CONV_K, tn), lambda j, i: (0, j)),
            pl.BlockSpec((FFN_CONV_K, tn), lambda j, i: (0, j + nj)),
        ],
        out_specs=pl.BlockSpec((tm, tn), lambda j, i: (i, j)),
        out_shape=jax.ShapeDtypeStruct((m, D_FF), BF16),
        scratch_shapes=[pltpu.VMEM((FFN_CARRY + tm, FFN_SUB), F32)] * (2 * (tn // FFN_SUB)),
        compiler_params=_cparams(("parallel", "arbitrary")),
        name="ffn_up_conv_gate",
    )(h, w_up, w_up, conv, conv)


CONF_HALO = 32
CONF_ROWS = 32
CONF_COLS = 256


def _conformer_kernel(val_ref, gate_ref, hval_ref, hgate_ref, cw_ref, cb_ref, lng_ref, lnb_ref,
                      o_ref, glu_s, u_s, *, tm, seq):
    i = pl.program_id(0)
    first = (i * tm) % seq == 0
    hglu = hval_ref[...] * jax.nn.sigmoid(hgate_ref[...])
    glu_s[0, 0:CONF_HALO, :] = jnp.where(first, jnp.zeros_like(hglu), hglu)
    glu_s[0, CONF_HALO:CONF_HALO + tm, :] = val_ref[...] * jax.nn.sigmoid(gate_ref[...])
    for b in range(1, SUBLANES):
        glu_s[b, SUBLANES:CONF_HALO + tm, :] = glu_s[0, SUBLANES - b:CONF_HALO + tm - b, :]
    for r0 in range(0, tm, CONF_ROWS):
        for c0 in range(0, CONV_W, CONF_COLS):
            acc = jnp.zeros((CONF_ROWS, CONF_COLS), F32) + cb_ref[:, c0:c0 + CONF_COLS]
            for k in range(CONV_K):
                a, b = divmod(CONV_K - 1 - k, SUBLANES)
                start = CONF_HALO + r0 - SUBLANES * a
                acc = acc + cw_ref[k:k + 1, c0:c0 + CONF_COLS] * glu_s[b, start:start + CONF_ROWS,
                                                                   c0:c0 + CONF_COLS]
            u_s[r0:r0 + CONF_ROWS, c0:c0 + CONF_COLS] = acc
    u = u_s[...]
    mean = jnp.mean(u, axis=-1, keepdims=True)
    d = u - mean
    var = jnp.mean(d * d, axis=-1, keepdims=True)
    y = d * lax.rsqrt(var + LN_EPS) * lng_ref[...] + lnb_ref[...]
    o_ref[...] = (y * jax.nn.sigmoid(y)).astype(o_ref.dtype)


def _conformer(p, cw, cb, lng, lnb, seq, tm=256):
    m = p.shape[0]
    vb, gb = COL_CVAL // CONV_W, COL_CGATE // CONV_W
    hmap = lambda i: jnp.maximum(i * (tm // CONF_HALO) - 1, 0)
    return pl.pallas_call(
        functools.partial(_conformer_kernel, tm=tm, seq=seq),
        grid=(m // tm,),
        in_specs=[
            pl.BlockSpec((tm, CONV_W), lambda i: (i, vb)),
            pl.BlockSpec((tm, CONV_W), lambda i: (i, gb)),
            pl.BlockSpec((CONF_HALO, CONV_W), lambda i: (hmap(i), vb)),
            pl.BlockSpec((CONF_HALO, CONV_W), lambda i: (hmap(i), gb)),
            pl.BlockSpec((CONV_K, CONV_W), lambda i: (0, 0)),
            pl.BlockSpec((1, CONV_W), lambda i: (0, 0)),
            pl.BlockSpec((1, CONV_W), lambda i: (0, 0)),
            pl.BlockSpec((1, CONV_W), lambda i: (0, 0)),
        ],
        out_specs=pl.BlockSpec((tm, CONV_W), lambda i: (i, 0)),
        out_shape=jax.ShapeDtypeStruct((m, CONV_W), BF16),
        scratch_shapes=[pltpu.VMEM((SUBLANES, CONF_HALO + tm, CONV_W), F32),
                        pltpu.VMEM((tm, CONV_W), F32)],
        compiler_params=_cparams(("parallel",)),
        name="conformer_conv",
    )(p, p, p, p, cw, cb.reshape(1, -1), lng.reshape(1, -1), lnb.reshape(1, -1))


SB_BQ = 1024
SB_BK = 256
SB_PREP_ROWS = 512
SB_GROUP = 8
LOG2E = 1.4426950408889634
LN2 = 0.6931471805599453


def _sb_kernel(q_ref, k_ref, v_ref, qg_ref, kg_ref, o_ref, kn_s, vb_s, *, seq, bq):
    qi = pl.program_id(2)

    @pl.when(qi == 0)
    def _():
        def body(c, carry):
            sl = pl.ds(pl.multiple_of(c * SB_PREP_ROWS, SB_PREP_ROWS), SB_PREP_ROWS)
            k = k_ref[sl, :]
            ms = jnp.mean(k * k, axis=-1, keepdims=True)
            kn_s[sl, :] = (k * lax.rsqrt(ms + NORM_EPS) * kg_ref[...]).astype(BF16)
            vb_s[sl, :] = v_ref[sl, :].astype(BF16)
            return carry
        lax.fori_loop(0, seq // SB_PREP_ROWS, body, 0)

    q = q_ref[...]
    ms = jnp.mean(q * q, axis=-1, keepdims=True)
    q = (q * lax.rsqrt(ms + NORM_EPS) * qg_ref[...] * (SB_HEAD ** -0.5 * LOG2E)).astype(BF16)

    row = lax.broadcasted_iota(jnp.int32, (SB_BK, SB_BK), 0)
    col = lax.broadcasted_iota(jnp.int32, (SB_BK, SB_BK), 1)
    later = (row > col).astype(BF16)
    causal = col < row

    def blocks(qs, k0s, nsub, cs, accs, masked):
        n = range(len(qs))
        sls = [pl.ds(pl.multiple_of(k0, SB_BK), nsub * SB_BK) for k0 in k0s]
        zs = [lax.dot_general(qs[i], kn_s[sls[i], :], (((1,), (1,)), ((), ())), preferred_element_type=F32)
              for i in n]
        sps = [jnp.maximum(z, 0.0) + jnp.log(1.0 + jnp.exp2(_neg_abs(z))) * (1.0 / LN2) for z in zs]
        if masked:
            sps = [jnp.where(causal, sp, 0.0) for sp in sps]
        atts = [[None] * nsub for _ in n]
        for j in reversed(range(nsub)):
            cols = slice(j * SB_BK, (j + 1) * SB_BK)
            sp_j = [sp[:, cols] for sp in sps]
            suffix = [_dot(x, later) for x in sp_j]
            att = [jnp.exp2((zs[i][:, cols] - sp_j[i]) - (suffix[i] + cs[i])) for i in n]
            if masked:
                att = [jnp.where(causal, x, 0.0) for x in att]
            for i in n:
                atts[i][j] = att[i].astype(BF16)
            cs = [cs[i] + jnp.sum(sp_j[i], axis=-1, keepdims=True) for i in n]
        accs = [accs[i] + jnp.dot(atts[i][0] if nsub == 1 else jnp.concatenate(atts[i], axis=1),
                                  vb_s[sls[i], :], preferred_element_type=F32) for i in n]
        return cs, accs

    ntile = bq // SB_BK
    tiles = range(ntile)
    qs = [q[h * SB_BK:(h + 1) * SB_BK, :] for h in tiles]
    cs = [jnp.zeros((SB_BK, 1), F32) for _ in tiles]
    accs = [jnp.zeros((SB_BK, SB_HEAD), F32) for _ in tiles]
    base = qi * bq
    cs, accs = blocks(qs, [base + h * SB_BK for h in tiles], 1, cs, accs, True)
    for s in range(1, ntile):
        sub = list(range(s, ntile))
        c1, a1 = blocks([qs[h] for h in sub], [base + (h - s) * SB_BK for h in sub], 1,
                        [cs[h] for h in sub], [accs[h] for h in sub], False)
        cs, accs = cs[:s] + c1, accs[:s] + a1

    def run(nsub, start):
        def body(n, carry):
            k0 = start - (n + 1) * nsub * SB_BK
            c, a = blocks(qs, [k0] * ntile, nsub, list(carry[0]), list(carry[1]), False)
            return tuple(c), tuple(a)
        return body

    before = qi * ntile
    rem = (before % SB_GROUP) // ntile
    carry = lax.fori_loop(0, rem, run(ntile, base), (tuple(cs), tuple(accs)))
    cs, accs = lax.fori_loop(0, before // SB_GROUP, run(SB_GROUP, base - rem * ntile * SB_BK), carry)
    for h in tiles:
        o_ref[h * SB_BK:(h + 1) * SB_BK, :] = accs[h].astype(o_ref.dtype)


def _stick_breaking(p, qg, kg, bsz, seq):
    bq = min(SB_BQ, seq)
    assert bq % SB_BK == 0 and SB_GROUP % (bq // SB_BK) == 0
    nq = seq // bq
    qb, kb, vb = COL_SBQ // SB_HEAD, COL_SBK // SB_HEAD, COL_SBV // SB_HEAD
    return pl.pallas_call(
        functools.partial(_sb_kernel, seq=seq, bq=bq),
        grid=(bsz, SB_HEADS, nq),
        in_specs=[
            pl.BlockSpec((bq, SB_HEAD), lambda b, h, q: (b * nq + q, qb + h)),
            pl.BlockSpec((seq, SB_HEAD), lambda b, h, q: (b, kb + h)),
            pl.BlockSpec((seq, SB_HEAD), lambda b, h, q: (b, vb + h)),
            pl.BlockSpec((1, SB_HEAD), lambda b, h, q: (0, 0)),
            pl.BlockSpec((1, SB_HEAD), lambda b, h, q: (0, 0)),
        ],
        out_specs=pl.BlockSpec((bq, SB_HEAD), lambda b, h, q: (b * nq + q, h)),
        out_shape=jax.ShapeDtypeStruct((bsz * seq, SB_W), BF16),
        scratch_shapes=[pltpu.VMEM((seq, SB_HEAD), BF16), pltpu.VMEM((seq, SB_HEAD), BF16)],
        compiler_params=_cparams(("parallel", "parallel", "arbitrary")),
        name="stick_breaking",
    )(p, p, p, qg.reshape(1, -1), kg.reshape(1, -1))


PREP_HALO = 8


def _head_ones():
    r = lax.broadcasted_iota(jnp.int32, (PAIR, PAIR), 0) // RWKV_HEAD
    c = lax.broadcasted_iota(jnp.int32, (PAIR, PAIR), 1) // RWKV_HEAD
    return (r == c).astype(BF16)


def _rwkv_prep_kernel(p_ref, lora_ref, hp_ref, hlora_ref, mu_ref, mul_ref, w0_ref, wup_ref, a0_ref,
                      aup_ref, gup_ref, kk_ref, ka_ref, rk_ref,
                      r_o, k_o, v_o, lw_o, a_o, b_o, bonus_o, g_o, *, first):
    def shifted(x_ref, h_ref, m_ref, c0, c1):
        x = x_ref[:, c0:c1]
        prev = pltpu.roll(x, 1, 0)
        row = lax.broadcasted_iota(jnp.int32, x.shape, 0)
        last = h_ref[PREP_HALO - 1:PREP_HALO, c0:c1]
        last = jnp.where(first, jnp.zeros_like(last), last)
        prev = jnp.where(row == 0, last, prev)
        return x + m_ref[:, c0:c1] * (prev - x)

    lo = shifted(lora_ref, hlora_ref, mul_ref, 0, LORA_COLS)
    w_lo = lo[:, 0:LORA_PAD]
    a_lo = lo[:, LORA_PAD:2 * LORA_PAD]
    g_lo = lo[:, 2 * LORA_PAD:]
    ones = _head_ones()

    for c0 in range(0, RWKV_W, PREP_COLS):
        c1 = c0 + PREP_COLS
        r = shifted(p_ref, hp_ref, mu_ref, c0, c1)
        k = shifted(p_ref, hp_ref, mu_ref, RWKV_W + c0, RWKV_W + c1)
        v = shifted(p_ref, hp_ref, mu_ref, 2 * RWKV_W + c0, 2 * RWKV_W + c1)
        y = w0_ref[:, c0:c1] + _dot(jnp.tanh(w_lo), wup_ref[:, c0:c1])
        log_w = -_softplus(-y) - 0.5
        lw_o[:, c0:c1] = -jnp.exp(log_w)
        iclr = jax.nn.sigmoid(a0_ref[:, c0:c1] + _dot(a_lo, aup_ref[:, c0:c1]))
        g_o[:, c0:c1] = _dot(jax.nn.sigmoid(g_lo), gup_ref[:, c0:c1])
        kk = k * kk_ref[:, c0:c1]
        kt = k * (1.0 + (iclr - 1.0) * ka_ref[:, c0:c1])
        rkr = r * kt * rk_ref[:, c0:c1]
        for t0 in range(0, PREP_COLS, PAIR):
            t1 = t0 + PAIR
            kk_t = kk[:, t0:t1]
            ss = _dot_x2(kk_t * kk_t, ones)
            kk_n = kk_t * lax.rsqrt(jnp.maximum(ss, 1e-24))
            a_o[:, c0 + t0:c0 + t1] = -kk_n
            b_o[:, c0 + t0:c0 + t1] = kk_n * iclr[:, t0:t1]
            bonus_o[:, c0 + t0:c0 + t1] = _dot(rkr[:, t0:t1], ones) * v[:, t0:t1]
        r_o[:, c0:c1] = r
        k_o[:, c0:c1] = kt
        v_o[:, c0:c1] = v


GN_ROWS = 256


QUAD_HEADS = 2
QUAD = QUAD_HEADS * RWKV_HEAD
SCAN_GROUPS = 16
SCAN_ROWS = 256


def _rwkv_quad_kernel(r_ref, k_ref, v_ref, lw_ref, a_ref, b_ref, bonus_ref, g_ref, gng_ref, gnb_ref,
                      o_ref, s_ref, rq_s, oin_s, tr_s, dl_s, *, tblk, t_id):
    C = RWKV_C
    HC = QUAD_HEADS * C
    G = range(SCAN_GROUPS)

    @pl.when(t_id == 0)
    def _():
        s_ref[...] = jnp.zeros_like(s_ref)

    lane = lax.broadcasted_iota(jnp.int32, (1, QUAD), 1) // RWKV_HEAD
    hm = [(lane == h).astype(F32) for h in range(QUAD_HEADS)]
    ti = lax.broadcasted_iota(jnp.int32, (C, C), 0)
    ii = lax.broadcasted_iota(jnp.int32, (C, C), 1)
    tri_incl = (ii <= ti).astype(BF16)
    t4 = lax.broadcasted_iota(jnp.int32, (C, HC), 0)
    i4 = lax.broadcasted_iota(jnp.int32, (C, HC), 1) % C
    strict4 = (i4 < t4).astype(F32)
    incl4 = (i4 <= t4).astype(F32)
    rr = lax.broadcasted_iota(jnp.int32, (QUAD, QUAD), 0)
    cc = lax.broadcasted_iota(jnp.int32, (QUAD, QUAD), 1)
    bdmask = (rr // RWKV_HEAD == cc // RWKV_HEAD).astype(F32)
    eye = (rr == cc).astype(F32)
    ones = (rr // RWKV_HEAD == cc // RWKV_HEAD).astype(BF16)

    def stack_heads(x):
        return jnp.concatenate([x * m for m in hm], axis=0)

    def stack_heads2(x, y):
        return jnp.concatenate([jnp.concatenate([x * m, y * m], axis=1) for m in hm], axis=0)

    def chunk(c, carry):
        sl = pl.ds(pl.multiple_of(c * C, C), C)
        lns = [slice(g * QUAD, (g + 1) * QUAD) for g in G]
        lw = [lw_ref[sl, ln] for ln in lns]
        cum = [_x3_dot(tri_incl, x) for x in lw]
        tot = [x[C - 1:C, :] for x in cum]
        e_neg = [jnp.exp(-x) for x in cum]
        e_end = [jnp.exp(t - x) for t, x in zip(tot, cum)]
        at = [a_ref[sl, ln] * jnp.exp(x - w) for ln, x, w in zip(lns, cum, lw)]
        rt = [r_ref[sl, ln] * jnp.exp(x) for ln, x in zip(lns, cum)]
        b = [b_ref[sl, ln] for ln in lns]
        k = [k_ref[sl, ln] for ln in lns]
        v = [v_ref[sl, ln] for ln in lns]
        gram = [_dot_nt(jnp.concatenate([at[g], rt[g]], axis=0),
                        jnp.concatenate([stack_heads(b[g] * e_neg[g]), stack_heads(k[g] * e_neg[g])], axis=0))
                for g in G]
        m_ak = [x[0:C, HC:] * strict4 for x in gram]
        q_b = [x[C:, 0:HC] * incl4 for x in gram]
        q_k = [x[C:, HC:] * incl4 for x in gram]
        vv = [stack_heads(x) for x in v]
        u_rhs = [_dot(m_ak[g], vv[g]) for g in G]
        pw = [stack_heads(x[0:C, 0:HC] * strict4) for x in gram]
        t_inv = [eye + x for x in pw]
        pw = [_dot(x, x) for x in pw]
        for _ in range(4):
            both = [_dot(jnp.concatenate([x, t], axis=0), x) for t, x in zip(t_inv, pw)]
            pw = [x[0:HC, :] for x in both]
            t_inv = [t + x[HC:, :] for t, x in zip(t_inv, both)]
        t_inv = [t + _dot(t, x) for t, x in zip(t_inv, pw)]
        t_ls = [sum(t[h * C:(h + 1) * C, :] for h in range(1, QUAD_HEADS)) + t[0:C, :] for t in t_inv]
        sol = [_dot(t_ls[g], stack_heads2(at[g], u_rhs[g])) for g in G]
        wa = [x[:, 0:QUAD] for x in sol]
        u0 = [x[:, QUAD:] for x in sol]
        y = [_dot(q_b[g], stack_heads2(wa[g], u0[g])) for g in G]
        o_in2 = [_dot(q_k[g], vv[g]) for g in G]
        bend = [b[g] * e_end[g] for g in G]
        tr = [_dot_tn(wa[g], bend[g]) for g in G]
        dl = [_dot_tn(jnp.concatenate([u0[g], v[g]], axis=0),
                      jnp.concatenate([bend[g], k[g] * e_end[g]], axis=0)) for g in G]
        for g in G:
            rq_s[sl, lns[g]] = rt[g] + y[g][:, 0:QUAD]
            oin_s[sl, lns[g]] = y[g][:, QUAD:] + o_in2[g]
            tr_s[g, c] = tr[g] * bdmask + eye * jnp.exp(tot[g])
            dl_s[g, c] = dl[g] * bdmask
        return carry

    lax.fori_loop(0, tblk // C, chunk, 0)

    def scan(c, carry):
        sl = pl.ds(pl.multiple_of(c * C, C), C)
        for g in G:
            ln = slice(g * QUAD, (g + 1) * QUAD)
            s = s_ref[g]
            oin_s[sl, ln] = _dot_nt(rq_s[sl, ln], s) + oin_s[sl, ln]
            s_ref[g] = _dot(s, tr_s[g, c]) + dl_s[g, c]
        return carry

    lax.fori_loop(0, tblk // C, scan, 0)

    for g in G:
        ln = slice(g * QUAD, (g + 1) * QUAD)
        for r0 in range(0, tblk, GN_ROWS):
            rows = slice(r0, r0 + GN_ROWS)
            o = oin_s[rows, ln]
            mean = _dot_x3(o, ones) * (1.0 / RWKV_HEAD)
            d = o - mean
            var = _dot_x2(d * d, ones) * (1.0 / RWKV_HEAD)
            on = d * lax.rsqrt(var + GN_EPS) * gng_ref[:, ln] + gnb_ref[:, ln]
            o_ref[rows, ln] = ((on + bonus_ref[rows, ln]) * g_ref[rows, ln]).astype(o_ref.dtype)


def _rwkv_fused_kernel(p_ref, lora_ref, hp_ref, hlora_ref, mu_ref, mul_ref, w0_ref, wup_ref, a0_ref,
                       aup_ref, gup_ref, kk_ref, ka_ref, rk_ref, gng_ref, gnb_ref, o_ref,
                       r_s, k_s, v_s, lw_s, a_s, b_s, bonus_s, g_s, s_ref, rq_s, oin_s, tr_s, dl_s, *, tblk):
    t_id = pl.program_id(1)
    _rwkv_prep_kernel(p_ref, lora_ref, hp_ref, hlora_ref, mu_ref, mul_ref, w0_ref, wup_ref, a0_ref,
                      aup_ref, gup_ref, kk_ref, ka_ref, rk_ref,
                      r_s, k_s, v_s, lw_s, a_s, b_s, bonus_s, g_s, first=t_id == 0)
    _rwkv_quad_kernel(r_s, k_s, v_s, lw_s, a_s, b_s, bonus_s, g_s, gng_ref, gnb_ref,
                      o_ref, s_ref, rq_s, oin_s, tr_s, dl_s, tblk=tblk, t_id=t_id)


def _rwkv_mix(p, mu, mul, w0, wup, a0, aup, gup, kk, ka, rk, gng, gnb, bsz, seq, tblk):
    assert SCAN_GROUPS * QUAD == RWKV_W and RWKV_C == RWKV_HEAD
    nt = seq // tblk
    nch = tblk // RWKV_C
    lb = COL_LORA // LORA_COLS
    blk = lambda b, t: b * nt + t
    hmap = lambda b, t: jnp.maximum(blk(b, t) * (tblk // PREP_HALO) - 1, 0)
    row = lambda a: a.reshape(1, -1)
    vec = pl.BlockSpec((1, RWKV_W), lambda b, t: (0, 0))
    full = lambda r, c: pl.BlockSpec((r, c), lambda b, t: (0, 0))
    act = pltpu.VMEM((tblk, RWKV_W), F32)
    return pl.pallas_call(
        functools.partial(_rwkv_fused_kernel, tblk=tblk),
        grid=(bsz, nt),
        in_specs=[
            pl.BlockSpec((tblk, 3 * RWKV_W), lambda b, t: (blk(b, t), 0)),
            pl.BlockSpec((tblk, LORA_COLS), lambda b, t: (blk(b, t), lb)),
            pl.BlockSpec((PREP_HALO, 3 * RWKV_W), lambda b, t: (hmap(b, t), 0)),
            pl.BlockSpec((PREP_HALO, LORA_COLS), lambda b, t: (hmap(b, t), lb)),
            full(1, 3 * RWKV_W), full(1, LORA_COLS),
            vec, full(LORA_PAD, RWKV_W), vec, full(LORA_PAD, RWKV_W), full(GATE_LORA, RWKV_W),
            vec, vec, vec, vec, vec,
        ],
        out_specs=pl.BlockSpec((tblk, RWKV_W), lambda b, t: (blk(b, t), 0)),
        out_shape=jax.ShapeDtypeStruct((bsz * seq, RWKV_W), BF16),
        scratch_shapes=[act] * 8 + [
            pltpu.VMEM((SCAN_GROUPS, QUAD, QUAD), F32), act, act,
            pltpu.VMEM((SCAN_GROUPS, nch, QUAD, QUAD), F32),
            pltpu.VMEM((SCAN_GROUPS, nch, QUAD, QUAD), F32)],
        compiler_params=_cparams(("parallel", "arbitrary")),
        name="rwkv_mix",
    )(p, p, p, p, row(mu), row(mul), row(w0), wup, row(a0), aup, gup, row(kk), row(ka), row(rk),
      row(gng), row(gnb))


def _relayout_cols(w, axis):
    def sl(a, b):
        idx = [slice(None)] * w.ndim
        idx[axis] = slice(a, b)
        return w[tuple(idx)]
    pad_shape = list(w.shape)
    pad_shape[axis] = LORA_PAD - DECAY_LORA
    z = jnp.zeros(pad_shape, w.dtype)
    lo = 3 * RWKV_W
    rest = lo + DECAY_LORA + ICLR_LORA + GATE_LORA
    return jnp.concatenate(
        [sl(0, lo), sl(rest, w.shape[axis]), sl(lo, lo + DECAY_LORA), z,
         sl(lo + DECAY_LORA, lo + DECAY_LORA + ICLR_LORA), z,
         sl(lo + DECAY_LORA + ICLR_LORA, rest)], axis=axis)


def _pad_rows(w):
    return jnp.concatenate([w, jnp.zeros((LORA_PAD - w.shape[0], w.shape[1]), w.dtype)], axis=0)


def _layer(x, bsz, seq, layer, w_in, w_out, attn_norm, ffn_norm, mu, w0, w_up, a0, a_up, g_up, k_k, k_a, r_k,
           gn_g, gn_b, conv_w, conv_b, ln_g, ln_b, q_norm, k_norm, ffn_up, ffn_conv, ffn_down, tm_mm=1024):
    h = _rmsnorm(x, attn_norm)
    p = _matmul_nt(h, w_in, layer, tm=tm_mm, name="proj_in")
    mu_p, mu_l = mu[:3 * RWKV_W], mu[COL_LORA:]
    y_a = _rwkv_mix(p, mu_p, mu_l, w0, w_up, a0, a_up, g_up, k_k, k_a, r_k.reshape(-1), gn_g, gn_b,
                    bsz, seq, tblk=min(SCAN_ROWS, seq))
    y_b = _conformer(p, conv_w, conv_b, ln_g, ln_b, seq)
    y_c = _stick_breaking(p, q_norm, k_norm, bsz, seq)
    x = _matmul([y_a, y_b, y_c], w_out, layer, residual=x, tm=tm_mm, name="proj_out")
    h = _rmsnorm(x, ffn_norm)
    act = _ffn_up(h, ffn_up, layer, ffn_conv, seq, tm=tm_mm)
    return _matmul([act], ffn_down, layer, residual=x, tm=tm_mm, tn=256, name="ffn_down")


def kernel(x, w_in, w_out, attn_norm, ffn_norm, rwkv_mu, rwkv_w0, rwkv_w_up, rwkv_a0, rwkv_a_up,
           rwkv_g_up, rwkv_k_k, rwkv_k_a, rwkv_r_k, rwkv_gn_g, rwkv_gn_b, conv_w, conv_b, conv_ln_g,
           conv_ln_b, sb_q_norm, sb_k_norm, ffn_up, ffn_conv, ffn_down):
    bsz, seq, d = x.shape
    x = x.reshape(bsz * seq, d)
    w_in_b = _relayout_cols(jnp.swapaxes(w_in, 1, 2).astype(BF16), 1)
    w_out_b, ffn_up_b, ffn_down_b = w_out.astype(BF16), ffn_up.astype(BF16), ffn_down.astype(BF16)
    for l in range(w_in.shape[0]):
        mu_full = jnp.concatenate([rwkv_mu[l], jnp.zeros((w_in.shape[2] - rwkv_mu.shape[1],), F32)])
        mu_re = _relayout_cols(mu_full, 0)
        x = _layer(
            x, bsz, seq, l, w_in_b, w_out_b, attn_norm[l], ffn_norm[l],
            mu_re, rwkv_w0[l], _pad_rows(rwkv_w_up[l]).astype(BF16), rwkv_a0[l],
            _pad_rows(rwkv_a_up[l]).astype(BF16), rwkv_g_up[l].astype(BF16), rwkv_k_k[l], rwkv_k_a[l],
            rwkv_r_k[l], rwkv_gn_g[l], rwkv_gn_b[l], conv_w[l], conv_b[l], conv_ln_g[l], conv_ln_b[l],
            sb_q_norm[l], sb_k_norm[l], ffn_up_b, ffn_conv[l], ffn_down_b,
            tm_mm=min(1024, seq))
    return x.reshape(bsz, seq, d)
```

```python
import functools

import jax
import jax.numpy as jnp
from jax import lax
from jax.experimental import pallas as pl
from jax.experimental.pallas import tpu as pltpu

F32 = jnp.float32
BF16 = jnp.bfloat16

D_MODEL = 4096
RWKV_HEAD = 64
RWKV_W = 2048
DECAY_LORA = 96
ICLR_LORA = 96
GATE_LORA = 256
CONV_W = 1024
CONV_K = 31
SB_HEAD = 128
SB_W = 1024
SB_HEADS = SB_W // SB_HEAD
D_FF = 2 * D_MODEL
FFN_CONV_K = 3
NORM_EPS = 1e-6
LN_EPS = 1e-5
GN_EPS = RWKV_HEAD * 1e-5

SUBLANES = 8
LORA_PAD = 128
COL_CVAL = 3 * RWKV_W
COL_CGATE = COL_CVAL + CONV_W
COL_SBQ = COL_CGATE + CONV_W
COL_SBK = COL_SBQ + SB_W
COL_SBV = COL_SBK + SB_W
COL_LORA = COL_SBV + SB_W
LORA_COLS = 2 * LORA_PAD + GATE_LORA
PROJ_PAD = COL_LORA + LORA_COLS
VMEM_LIMIT = 56 * 1024 * 1024

RWKV_C = 64
PAIR = 2 * RWKV_HEAD
PREP_COLS = 512


def _cparams(sem):
    return pltpu.CompilerParams(dimension_semantics=sem, vmem_limit_bytes=VMEM_LIMIT)


def _dot(a, b):
    return jnp.dot(a.astype(BF16), b.astype(BF16), preferred_element_type=F32)


def _dot_nt(a, b):
    return lax.dot_general(a.astype(BF16), b.astype(BF16), (((1,), (1,)), ((), ())),
                           preferred_element_type=F32)


def _dot_tn(a, b):
    return lax.dot_general(a.astype(BF16), b.astype(BF16), (((0,), (0,)), ((), ())),
                           preferred_element_type=F32)


def _split3(x):
    hi = x.astype(BF16)
    r1 = x - hi.astype(F32)
    mid = r1.astype(BF16)
    lo = (r1 - mid.astype(F32)).astype(BF16)
    return hi, mid, lo


def _dot_x3(a, b_exact):
    hi, mid, lo = _split3(a)
    b = b_exact.astype(BF16)
    return (jnp.dot(hi, b, preferred_element_type=F32)
            + jnp.dot(mid, b, preferred_element_type=F32)
            + jnp.dot(lo, b, preferred_element_type=F32))


def _x3_dot(a_exact, b):
    hi, mid, lo = _split3(b)
    a = a_exact.astype(BF16)
    return (jnp.dot(a, hi, preferred_element_type=F32)
            + jnp.dot(a, mid, preferred_element_type=F32)
            + jnp.dot(a, lo, preferred_element_type=F32))


def _dot_x2(a, b_exact):
    hi = a.astype(BF16)
    lo = (a - hi.astype(F32)).astype(BF16)
    b = b_exact.astype(BF16)
    return jnp.dot(hi, b, preferred_element_type=F32) + jnp.dot(lo, b, preferred_element_type=F32)


def _neg_abs(x):
    bits = lax.bitcast_convert_type(x, jnp.int32) | jnp.int32(-2 ** 31)
    return lax.bitcast_convert_type(bits, F32)


def _softplus(x):
    return jnp.maximum(x, 0.0) + jnp.log(1.0 + jnp.exp(_neg_abs(x)))


def _rmsnorm_kernel(x_ref, g_ref, o_ref):
    x = x_ref[...]
    ms = jnp.mean(x * x, axis=-1, keepdims=True)
    o_ref[...] = (x * lax.rsqrt(ms + NORM_EPS) * g_ref[...]).astype(o_ref.dtype)


def _rmsnorm(x, g, tm=512):
    m, d = x.shape
    return pl.pallas_call(
        _rmsnorm_kernel,
        grid=(m // tm,),
        in_specs=[pl.BlockSpec((tm, d), lambda i: (i, 0)),
                  pl.BlockSpec((1, d), lambda i: (0, 0))],
        out_specs=pl.BlockSpec((tm, d), lambda i: (i, 0)),
        out_shape=jax.ShapeDtypeStruct((m, d), BF16),
        compiler_params=_cparams(("parallel",)),
        name="rmsnorm",
    )(x, g.reshape(1, d))


def _mm_kernel(*refs, n_lhs, has_res):
    o_ref = refs[-1]
    acc = None
    for a_ref, w_ref in zip(refs[:n_lhs], refs[n_lhs:2 * n_lhs]):
        d = jnp.dot(a_ref[...], w_ref[...], preferred_element_type=F32)
        acc = d if acc is None else acc + d
    if has_res:
        acc = acc + refs[2 * n_lhs][...]
    o_ref[...] = acc.astype(o_ref.dtype)


def _mm_nt_kernel(a_ref, w_ref, o_ref):
    o_ref[...] = lax.dot_general(a_ref[...], w_ref[...], (((1,), (1,)), ((), ())),
                                 preferred_element_type=F32)


def _matmul_nt(a, w_t, layer, tm=1024, tn=512, name="matmul_nt"):
    m, k = a.shape
    n = w_t.shape[1]
    return pl.pallas_call(
        _mm_nt_kernel,
        grid=(m // tm, n // tn),
        in_specs=[pl.BlockSpec((tm, k), lambda i, j: (i, 0)),
                  pl.BlockSpec((pl.Squeezed(), tn, k), lambda i, j: (layer, j, 0))],
        out_specs=pl.BlockSpec((tm, tn), lambda i, j: (i, j)),
        out_shape=jax.ShapeDtypeStruct((m, n), F32),
        compiler_params=_cparams(("parallel", "arbitrary")),
        name=name,
    )(a, w_t)


def _matmul(lhs_list, w, layer, residual=None, tm=1024, tn=512, name="matmul"):
    m = lhs_list[0].shape[0]
    n = w.shape[2]
    in_specs, args = [], []
    for a in lhs_list:
        in_specs.append(pl.BlockSpec((tm, a.shape[1]), lambda i, j: (i, 0)))
        args.append(a)
    row = 0
    for a in lhs_list:
        k = a.shape[1]
        assert row % k == 0
        in_specs.append(pl.BlockSpec((pl.Squeezed(), k, tn),
                                     functools.partial(lambda i, j, rb: (layer, rb, j), rb=row // k)))
        args.append(w)
        row += k
    assert row == w.shape[1]
    if residual is not None:
        in_specs.append(pl.BlockSpec((tm, tn), lambda i, j: (i, j)))
        args.append(residual)
    return pl.pallas_call(
        functools.partial(_mm_kernel, n_lhs=len(lhs_list), has_res=residual is not None),
        grid=(m // tm, n // tn),
        in_specs=in_specs,
        out_specs=pl.BlockSpec((tm, tn), lambda i, j: (i, j)),
        out_shape=jax.ShapeDtypeStruct((m, n), F32),
        compiler_params=_cparams(("parallel", "arbitrary")),
        name=name,
    )(*args)


FFN_CARRY = 8
FFN_SUB = 256


def _ffn_up_kernel(h_ref, wg_ref, wv_ref, cg_ref, cv_ref, o_ref, *scratch, tm, tn, seq):
    i = pl.program_id(1)
    first = (i * tm) % seq == 0
    nsub = tn // FFN_SUB
    sg, sv = scratch[:nsub], scratch[nsub:]

    @pl.when(first)
    def _():
        for s in scratch:
            s[0:FFN_CARRY, :] = jnp.zeros((FFN_CARRY, FFN_SUB), F32)

    @pl.when(jnp.logical_not(first))
    def _():
        for s in scratch:
            s[0:FFN_CARRY, :] = s[tm:tm + FFN_CARRY, :]

    def up(w_ref, s, n):
        s[FFN_CARRY:FFN_CARRY + tm, :] = jnp.dot(h_ref[...], w_ref[:, n * FFN_SUB:(n + 1) * FFN_SUB],
                                                 preferred_element_type=F32)

    def conv(c_ref, s, n):
        acc = None
        for k in range(FFN_CONV_K):
            off = FFN_CARRY - (FFN_CONV_K - 1) + k
            term = c_ref[k:k + 1, n * FFN_SUB:(n + 1) * FFN_SUB] * s[off:off + tm, :]
            acc = term if acc is None else acc + term
        return acc

    def gate(n):
        g = conv(cg_ref, sg[n], n)
        v = conv(cv_ref, sv[n], n)
        silu = 0.5 * g * (1.0 + jnp.tanh(0.5 * g))
        o_ref[:, n * FFN_SUB:(n + 1) * FFN_SUB] = (silu * v).astype(o_ref.dtype)

    up(wg_ref, sg[0], 0)
    up(wv_ref, sv[0], 0)
    for n in range(1, nsub):
        up(wg_ref, sg[n], n)
        gate(n - 1)
        up(wv_ref, sv[n], n)
    gate(nsub - 1)


def _ffn_up(h, w_up, layer, conv, seq, tm=1024, tn=512):
    m, d = h.shape
    nj = D_FF // tn
    return pl.pallas_call(
        functools.partial(_ffn_up_kernel, tm=tm, tn=tn, seq=seq),
        grid=(nj, m // tm),
        in_specs=[
            pl.BlockSpec((tm, d), lambda j, i: (i, 0)),
            pl.BlockSpec((pl.Squeezed(), d, tn), lambda j, i: (layer, 0, j)),
            pl.BlockSpec((pl.Squeezed(), d, tn), lambda j, i: (layer, 0, j + nj)),
            pl.BlockSpec((FFN_CONV_K, tn), lambda j, i: (0, j)),
            pl.BlockSpec((FFN_CONV_K, tn), lambda j, i: (0, j + nj)),
        ],
        out_specs=pl.BlockSpec((tm, tn), lambda j, i: (i, j)),
        out_shape=jax.ShapeDtypeStruct((m, D_FF), BF16),
        scratch_shapes=[pltpu.VMEM((FFN_CARRY + tm, FFN_SUB), F32)] * (2 * (tn // FFN_SUB)),
        compiler_params=_cparams(("parallel", "arbitrary")),
        name="ffn_up_conv_gate",
    )(h, w_up, w_up, conv, conv)


CONF_HALO = 32
CONF_ROWS = 32
CONF_COLS = 256


def _conformer_kernel(val_ref, gate_ref, hval_ref, hgate_ref, cw_ref, cb_ref, lng_ref, lnb_ref,
                      o_ref, glu_s, u_s, *, tm, seq):
    i = pl.program_id(0)
    first = (i * tm) % seq == 0
    hglu = hval_ref[...] * jax.nn.sigmoid(hgate_ref[...])
    glu_s[0, 0:CONF_HALO, :] = jnp.where(first, jnp.zeros_like(hglu), hglu)
    glu_s[0, CONF_HALO:CONF_HALO + tm, :] = val_ref[...] * jax.nn.sigmoid(gate_ref[...])
    for b in range(1, SUBLANES):
        glu_s[b, SUBLANES:CONF_HALO + tm, :] = glu_s[0, SUBLANES - b:CONF_HALO + tm - b, :]
    for r0 in range(0, tm, CONF_ROWS):
        for c0 in range(0, CONV_W, CONF_COLS):
            acc = jnp.zeros((CONF_ROWS, CONF_COLS), F32) + cb_ref[:, c0:c0 + CONF_COLS]
            for k in range(CONV_K):
                a, b = divmod(CONV_K - 1 - k, SUBLANES)
                start = CONF_HALO + r0 - SUBLANES * a
                acc = acc + cw_ref[k:k + 1, c0:c0 + CONF_COLS] * glu_s[b, start:start + CONF_ROWS,
                                                                   c0:c0 + CONF_COLS]
            u_s[r0:r0 + CONF_ROWS, c0:c0 + CONF_COLS] = acc
    u = u_s[...]
    mean = jnp.mean(u, axis=-1, keepdims=True)
    d = u - mean
    var = jnp.mean(d * d, axis=-1, keepdims=True)
    y = d * lax.rsqrt(var + LN_EPS) * lng_ref[...] + lnb_ref[...]
    o_ref[...] = (y * jax.nn.sigmoid(y)).astype(o_ref.dtype)


def _conformer(p, cw, cb, lng, lnb, seq, tm=256):
    m = p.shape[0]
    vb, gb = COL_CVAL // CONV_W, COL_CGATE // CONV_W
    hmap = lambda i: jnp.maximum(i * (tm // CONF_HALO) - 1, 0)
    return pl.pallas_call(
        functools.partial(_conformer_kernel, tm=tm, seq=seq),
        grid=(m // tm,),
        in_specs=[
            pl.BlockSpec((tm, CONV_W), lambda i: (i, vb)),
            pl.BlockSpec((tm, CONV_W), lambda i: (i, gb)),
            pl.BlockSpec((CONF_HALO, CONV_W), lambda i: (hmap(i), vb)),
            pl.BlockSpec((CONF_HALO, CONV_W), lambda i: (hmap(i), gb)),
            pl.BlockSpec((CONV_K, CONV_W), lambda i: (0, 0)),
            pl.BlockSpec((1, CONV_W), lambda i: (0, 0)),
            pl.BlockSpec((1, CONV_W), lambda i: (0, 0)),
            pl.BlockSpec((1, CONV_W), lambda i: (0, 0)),
        ],
        out_specs=pl.BlockSpec((tm, CONV_W), lambda i: (i, 0)),
        out_shape=jax.ShapeDtypeStruct((m, CONV_W), BF16),
        scratch_shapes=[pltpu.VMEM((SUBLANES, CONF_HALO + tm, CONV_W), F32),
                        pltpu.VMEM((tm, CONV_W), F32)],
        compiler_params=_cparams(("parallel",)),
        name="conformer_conv",
    )(p, p, p, p, cw, cb.reshape(1, -1), lng.reshape(1, -1), lnb.reshape(1, -1))


SB_BQ = 1024
SB_BK = 256
SB_PREP_ROWS = 512
SB_GROUP = 8
LOG2E = 1.4426950408889634
LN2 = 0.6931471805599453


def _sb_kernel(q_ref, k_ref, v_ref, qg_ref, kg_ref, o_ref, kn_s, vb_s, *, seq, bq):
    qi = pl.program_id(2)

    @pl.when(qi == 0)
    def _():
        def body(c, carry):
            sl = pl.ds(pl.multiple_of(c * SB_PREP_ROWS, SB_PREP_ROWS), SB_PREP_ROWS)
            k = k_ref[sl, :]
            ms = jnp.mean(k * k, axis=-1, keepdims=True)
            kn_s[sl, :] = (k * lax.rsqrt(ms + NORM_EPS) * kg_ref[...]).astype(BF16)
            vb_s[sl, :] = v_ref[sl, :].astype(BF16)
            return carry
        lax.fori_loop(0, seq // SB_PREP_ROWS, body, 0)

    q = q_ref[...]
    ms = jnp.mean(q * q, axis=-1, keepdims=True)
    q = (q * lax.rsqrt(ms + NORM_EPS) * qg_ref[...] * (SB_HEAD ** -0.5 * LOG2E)).astype(BF16)

    row = lax.broadcasted_iota(jnp.int32, (SB_BK, SB_BK), 0)
    col = lax.broadcasted_iota(jnp.int32, (SB_BK, SB_BK), 1)
    later = (row > col).astype(BF16)
    causal = col < row

    def blocks(qs, k0s, nsub, cs, accs, masked):
        n = range(len(qs))
        sls = [pl.ds(pl.multiple_of(k0, SB_BK), nsub * SB_BK) for k0 in k0s]
        zs = [lax.dot_general(qs[i], kn_s[sls[i], :], (((1,), (1,)), ((), ())), preferred_element_type=F32)
              for i in n]
        sps = [jnp.maximum(z, 0.0) + jnp.log(1.0 + jnp.exp2(_neg_abs(z))) * (1.0 / LN2) for z in zs]
        if masked:
            sps = [jnp.where(causal, sp, 0.0) for sp in sps]
        atts = [[None] * nsub for _ in n]
        for j in reversed(range(nsub)):
            cols = slice(j * SB_BK, (j + 1) * SB_BK)
            sp_j = [sp[:, cols] for sp in sps]
            suffix = [_dot(x, later) for x in sp_j]
            att = [jnp.exp2((zs[i][:, cols] - sp_j[i]) - (suffix[i] + cs[i])) for i in n]
            if masked:
                att = [jnp.where(causal, x, 0.0) for x in att]
            for i in n:
                atts[i][j] = att[i].astype(BF16)
            cs = [cs[i] + jnp.sum(sp_j[i], axis=-1, keepdims=True) for i in n]
        accs = [accs[i] + jnp.dot(atts[i][0] if nsub == 1 else jnp.concatenate(atts[i], axis=1),
                                  vb_s[sls[i], :], preferred_element_type=F32) for i in n]
        return cs, accs

    ntile = bq // SB_BK
    tiles = range(ntile)
    qs = [q[h * SB_BK:(h + 1) * SB_BK, :] for h in tiles]
    cs = [jnp.zeros((SB_BK, 1), F32) for _ in tiles]
    accs = [jnp.zeros((SB_BK, SB_HEAD), F32) for _ in tiles]
    base = qi * bq
    cs, accs = blocks(qs, [base + h * SB_BK for h in tiles], 1, cs, accs, True)
    for s in range(1, ntile):
        sub = list(range(s, ntile))
        c1, a1 = blocks([qs[h] for h in sub], [base + (h - s) * SB_BK for h in sub], 1,
                        [cs[h] for h in sub], [accs[h] for h in sub], False)
        cs, accs = cs[:s] + c1, accs[:s] + a1

    def run(nsub, start):
        def body(n, carry):
            k0 = start - (n + 1) * nsub * SB_BK
            c, a = blocks(qs, [k0] * ntile, nsub, list(carry[0]), list(carry[1]), False)
            return tuple(c), tuple(a)
        return body

    before = qi * ntile
    rem = (before % SB_GROUP) // ntile
    carry = lax.fori_loop(0, rem, run(ntile, base), (tuple(cs), tuple(accs)))
    cs, accs = lax.fori_loop(0, before // SB_GROUP, run(SB_GROUP, base - rem * ntile * SB_BK), carry)
    for h in tiles:
        o_ref[h * SB_BK:(h + 1) * SB_BK, :] = accs[h].astype(o_ref.dtype)


def _stick_breaking(p, qg, kg, bsz, seq):
    bq = min(SB_BQ, seq)
    assert bq % SB_BK == 0 and SB_GROUP % (bq // SB_BK) == 0
    nq = seq // bq
    qb, kb, vb = COL_SBQ // SB_HEAD, COL_SBK // SB_HEAD, COL_SBV // SB_HEAD
    return pl.pallas_call(
        functools.partial(_sb_kernel, seq=seq, bq=bq),
        grid=(bsz, SB_HEADS, nq),
        in_specs=[
            pl.BlockSpec((bq, SB_HEAD), lambda b, h, q: (b * nq + q, qb + h)),
            pl.BlockSpec((seq, SB_HEAD), lambda b, h, q: (b, kb + h)),
            pl.BlockSpec((seq, SB_HEAD), lambda b, h, q: (b, vb + h)),
            pl.BlockSpec((1, SB_HEAD), lambda b, h, q: (0, 0)),
            pl.BlockSpec((1, SB_HEAD), lambda b, h, q: (0, 0)),
        ],
        out_specs=pl.BlockSpec((bq, SB_HEAD), lambda b, h, q: (b * nq + q, h)),
        out_shape=jax.ShapeDtypeStruct((bsz * seq, SB_W), BF16),
        scratch_shapes=[pltpu.VMEM((seq, SB_HEAD), BF16), pltpu.VMEM((seq, SB_HEAD), BF16)],
        compiler_params=_cparams(("parallel", "parallel", "arbitrary")),
        name="stick_breaking",
    )(p, p, p, qg.reshape(1, -1), kg.reshape(1, -1))


PREP_HALO = 8


def _head_ones():
    r = lax.broadcasted_iota(jnp.int32, (PAIR, PAIR), 0) // RWKV_HEAD
    c = lax.broadcasted_iota(jnp.int32, (PAIR, PAIR), 1) // RWKV_HEAD
    return (r == c).astype(BF16)


def _rwkv_prep_kernel(p_ref, lora_ref, hp_ref, hlora_ref, mu_ref, mul_ref, w0_ref, wup_ref, a0_ref,
                      aup_ref, gup_ref, kk_ref, ka_ref, rk_ref,
                      r_o, k_o, v_o, lw_o, a_o, b_o, bonus_o, g_o, *, first):
    def shifted(x_ref, h_ref, m_ref, c0, c1):
        x = x_ref[:, c0:c1]
        prev = pltpu.roll(x, 1, 0)
        row = lax.broadcasted_iota(jnp.int32, x.shape, 0)
        last = h_ref[PREP_HALO - 1:PREP_HALO, c0:c1]
        last = jnp.where(first, jnp.zeros_like(last), last)
        prev = jnp.where(row == 0, last, prev)
        return x + m_ref[:, c0:c1] * (prev - x)

    lo = shifted(lora_ref, hlora_ref, mul_ref, 0, LORA_COLS)
    w_act = jnp.tanh(lo[:, 0:LORA_PAD]).astype(BF16)
    a_act = lo[:, LORA_PAD:2 * LORA_PAD].astype(BF16)
    g_act = jax.nn.sigmoid(lo[:, 2 * LORA_PAD:]).astype(BF16)
    ones = _head_ones()

    for c0 in range(0, RWKV_W, PREP_COLS):
        c1 = c0 + PREP_COLS
        r = shifted(p_ref, hp_ref, mu_ref, c0, c1)
        k = shifted(p_ref, hp_ref, mu_ref, RWKV_W + c0, RWKV_W + c1)
        v = shifted(p_ref, hp_ref, mu_ref, 2 * RWKV_W + c0, 2 * RWKV_W + c1)
        y = w0_ref[:, c0:c1] + _dot(w_act, wup_ref[:, c0:c1])
        log_w = -_softplus(-y) - 0.5
        lw_o[:, c0:c1] = -jnp.exp(log_w)
        iclr = jax.nn.sigmoid(a0_ref[:, c0:c1] + _dot(a_act, aup_ref[:, c0:c1]))
        g_o[:, c0:c1] = _dot(g_act, gup_ref[:, c0:c1])
        kk = k * kk_ref[:, c0:c1]
        kt = k * (1.0 + (iclr - 1.0) * ka_ref[:, c0:c1])
        rkr = r * kt * rk_ref[:, c0:c1]
        for t0 in range(0, PREP_COLS, PAIR):
            t1 = t0 + PAIR
            kk_t = kk[:, t0:t1]
            ss = _dot_x2(kk_t * kk_t, ones)
            kk_n = kk_t * lax.rsqrt(jnp.maximum(ss, 1e-24))
            a_o[:, c0 + t0:c0 + t1] = -kk_n
            b_o[:, c0 + t0:c0 + t1] = kk_n * iclr[:, t0:t1]
            bonus_o[:, c0 + t0:c0 + t1] = _dot(rkr[:, t0:t1], ones) * v[:, t0:t1]
        r_o[:, c0:c1] = r
        k_o[:, c0:c1] = kt
        v_o[:, c0:c1] = v


GN_ROWS = 256


QUAD_HEADS = 2
QUAD = QUAD_HEADS * RWKV_HEAD
SCAN_GROUPS = 16
SCAN_ROWS = 256


def _rwkv_quad_kernel(r_ref, k_ref, v_ref, lw_ref, a_ref, b_ref, bonus_ref, g_ref, gng_ref, gnb_ref,
                      o_ref, s_ref, rq_s, oin_s, tr_s, dl_s, *, tblk, t_id):
    C = RWKV_C
    HC = QUAD_HEADS * C
    G = range(SCAN_GROUPS)

    @pl.when(t_id == 0)
    def _():
        s_ref[...] = jnp.zeros_like(s_ref)

    lane = lax.broadcasted_iota(jnp.int32, (1, QUAD), 1) // RWKV_HEAD
    hm = [(lane == h).astype(F32) for h in range(QUAD_HEADS)]
    ti = lax.broadcasted_iota(jnp.int32, (C, C), 0)
    ii = lax.broadcasted_iota(jnp.int32, (C, C), 1)
    tri_incl = (ii <= ti).astype(BF16)
    t4 = lax.broadcasted_iota(jnp.int32, (C, HC), 0)
    i4 = lax.broadcasted_iota(jnp.int32, (C, HC), 1) % C
    strict4 = (i4 < t4).astype(F32)
    incl4 = (i4 <= t4).astype(F32)
    mask_ak_qk = jnp.concatenate([strict4, incl4], axis=0)
    rr = lax.broadcasted_iota(jnp.int32, (QUAD, QUAD), 0)
    cc = lax.broadcasted_iota(jnp.int32, (QUAD, QUAD), 1)
    bdmask = (rr // RWKV_HEAD == cc // RWKV_HEAD).astype(F32)
    eye = (rr == cc).astype(F32)
    ones = (rr // RWKV_HEAD == cc // RWKV_HEAD).astype(BF16)

    def stack_heads(x):
        return jnp.concatenate([x * m for m in hm], axis=0)

    def stack_heads2(x, y):
        return jnp.concatenate([jnp.concatenate([x * m, y * m], axis=1) for m in hm], axis=0)

    def chunk(c, carry):
        sl = pl.ds(pl.multiple_of(c * C, C), C)
        lns = [slice(g * QUAD, (g + 1) * QUAD) for g in G]
        lw = [lw_ref[sl, ln] for ln in lns]
        cum = [_x3_dot(tri_incl, x) for x in lw]
        tot = [x[C - 1:C, :] for x in cum]
        e_neg = [jnp.exp(-x) for x in cum]
        e_end = [jnp.exp(t - x) for t, x in zip(tot, cum)]
        at = [a_ref[sl, ln] * jnp.exp(x - w) for ln, x, w in zip(lns, cum, lw)]
        rt = [r_ref[sl, ln] * jnp.exp(x) for ln, x in zip(lns, cum)]
        b = [b_ref[sl, ln] for ln in lns]
        k = [k_ref[sl, ln] for ln in lns]
        v = [v_ref[sl, ln] for ln in lns]
        gram = [_dot_nt(jnp.concatenate([at[g], rt[g]], axis=0),
                        jnp.concatenate([stack_heads(b[g] * e_neg[g]), stack_heads(k[g] * e_neg[g])], axis=0))
                for g in G]
        q_b = [x[C:, 0:HC] * incl4 for x in gram]
        vv = [stack_heads(x) for x in v]
        mv = [_dot(x[:, HC:] * mask_ak_qk, vv[g]) for g, x in enumerate(gram)]
        u_rhs = [x[0:C, :] for x in mv]
        o_in2 = [x[C:, :] for x in mv]
        pw = [stack_heads(x[0:C, 0:HC] * strict4) for x in gram]
        t_inv = [eye + x for x in pw]
        pw = [_dot(x, x) for x in pw]
        for _ in range(4):
            both = [_dot(jnp.concatenate([x, t], axis=0), x) for t, x in zip(t_inv, pw)]
            pw = [x[0:HC, :] for x in both]
            t_inv = [t + x[HC:, :] for t, x in zip(t_inv, both)]
        t_inv = [t + _dot(t, x) for t, x in zip(t_inv, pw)]
        t_ls = [sum(t[h * C:(h + 1) * C, :] for h in range(1, QUAD_HEADS)) + t[0:C, :] for t in t_inv]
        sol = [_dot(t_ls[g], stack_heads2(at[g], u_rhs[g])) for g in G]
        wa = [x[:, 0:QUAD] for x in sol]
        u0 = [x[:, QUAD:] for x in sol]
        y = [_dot(q_b[g], stack_heads2(wa[g], u0[g])) for g in G]
        td = [_dot_tn(jnp.concatenate([sol[g], jnp.concatenate([jnp.zeros_like(v[g]), v[g]], axis=1)], axis=0),
                      jnp.concatenate([b[g] * e_end[g], k[g] * e_end[g]], axis=0)) for g in G]
        for g in G:
            rq_s[sl, lns[g]] = rt[g] + y[g][:, 0:QUAD]
            oin_s[sl, lns[g]] = y[g][:, QUAD:] + o_in2[g]
            tr_s[g, c] = td[g][0:QUAD, :] * bdmask + eye * jnp.exp(tot[g])
            dl_s[g, c] = td[g][QUAD:, :] * bdmask
        return carry

    lax.fori_loop(0, tblk // C, chunk, 0)

    def scan(c, carry):
        sl = pl.ds(pl.multiple_of(c * C, C), C)
        for g in G:
            ln = slice(g * QUAD, (g + 1) * QUAD)
            s = s_ref[g]
            oin_s[sl, ln] = _dot_nt(rq_s[sl, ln], s) + oin_s[sl, ln]
            s_ref[g] = _dot(s, tr_s[g, c]) + dl_s[g, c]
        return carry

    lax.fori_loop(0, tblk // C, scan, 0)

    for g in G:
        ln = slice(g * QUAD, (g + 1) * QUAD)
        for r0 in range(0, tblk, GN_ROWS):
            rows = slice(r0, r0 + GN_ROWS)
            o = oin_s[rows, ln]
            mean = _dot_x3(o, ones) * (1.0 / RWKV_HEAD)
            d = o - mean
            var = _dot_x2(d * d, ones) * (1.0 / RWKV_HEAD)
            on = d * lax.rsqrt(var + GN_EPS) * gng_ref[:, ln] + gnb_ref[:, ln]
            o_ref[rows, ln] = ((on + bonus_ref[rows, ln]) * g_ref[rows, ln]).astype(o_ref.dtype)


def _rwkv_fused_kernel(p_ref, lora_ref, hp_ref, hlora_ref, mu_ref, mul_ref, w0_ref, wup_ref, a0_ref,
                       aup_ref, gup_ref, kk_ref, ka_ref, rk_ref, gng_ref, gnb_ref, o_ref,
                       r_s, k_s, v_s, lw_s, a_s, b_s, bonus_s, g_s, s_ref, rq_s, oin_s, tr_s, dl_s, *, tblk):
    t_id = pl.program_id(1)
    _rwkv_prep_kernel(p_ref, lora_ref, hp_ref, hlora_ref, mu_ref, mul_ref, w0_ref, wup_ref, a0_ref,
                      aup_ref, gup_ref, kk_ref, ka_ref, rk_ref,
                      r_s, k_s, v_s, lw_s, a_s, b_s, bonus_s, g_s, first=t_id == 0)
    _rwkv_quad_kernel(r_s, k_s, v_s, lw_s, a_s, b_s, bonus_s, g_s, gng_ref, gnb_ref,
                      o_ref, s_ref, rq_s, oin_s, tr_s, dl_s, tblk=tblk, t_id=t_id)


def _rwkv_mix(p, mu, mul, w0, wup, a0, aup, gup, kk, ka, rk, gng, gnb, bsz, seq, tblk):
    assert SCAN_GROUPS * QUAD == RWKV_W and RWKV_C == RWKV_HEAD
    nt = seq // tblk
    nch = tblk // RWKV_C
    lb = COL_LORA // LORA_COLS
    blk = lambda b, t: b * nt + t
    hmap = lambda b, t: jnp.maximum(blk(b, t) * (tblk // PREP_HALO) - 1, 0)
    row = lambda a: a.reshape(1, -1)
    vec = pl.BlockSpec((1, RWKV_W), lambda b, t: (0, 0))
    full = lambda r, c: pl.BlockSpec((r, c), lambda b, t: (0, 0))
    act = pltpu.VMEM((tblk, RWKV_W), F32)
    return pl.pallas_call(
        functools.partial(_rwkv_fused_kernel, tblk=tblk),
        grid=(bsz, nt),
        in_specs=[
            pl.BlockSpec((tblk, 3 * RWKV_W), lambda b, t: (blk(b, t), 0)),
            pl.BlockSpec((tblk, LORA_COLS), lambda b, t: (blk(b, t), lb)),
            pl.BlockSpec((PREP_HALO, 3 * RWKV_W), lambda b, t: (hmap(b, t), 0)),
            pl.BlockSpec((PREP_HALO, LORA_COLS), lambda b, t: (hmap(b, t), lb)),
            full(1, 3 * RWKV_W), full(1, LORA_COLS),
            vec, full(LORA_PAD, RWKV_W), vec, full(LORA_PAD, RWKV_W), full(GATE_LORA, RWKV_W),
            vec, vec, vec, vec, vec,
        ],
        out_specs=pl.BlockSpec((tblk, RWKV_W), lambda b, t: (blk(b, t), 0)),
        out_shape=jax.ShapeDtypeStruct((bsz * seq, RWKV_W), BF16),
        scratch_shapes=[act] * 8 + [
            pltpu.VMEM((SCAN_GROUPS, QUAD, QUAD), F32), act, act,
            pltpu.VMEM((SCAN_GROUPS, nch, QUAD, QUAD), F32),
            pltpu.VMEM((SCAN_GROUPS, nch, QUAD, QUAD), F32)],
        compiler_params=_cparams(("parallel", "arbitrary")),
        name="rwkv_mix",
    )(p, p, p, p, row(mu), row(mul), row(w0), wup, row(a0), aup, gup, row(kk), row(ka), row(rk),
      row(gng), row(gnb))


def _relayout_cols(w, axis):
    def sl(a, b):
        idx = [slice(None)] * w.ndim
        idx[axis] = slice(a, b)
        return w[tuple(idx)]
    pad_shape = list(w.shape)
    pad_shape[axis] = LORA_PAD - DECAY_LORA
    z = jnp.zeros(pad_shape, w.dtype)
    lo = 3 * RWKV_W
    rest = lo + DECAY_LORA + ICLR_LORA + GATE_LORA
    return jnp.concatenate(
        [sl(0, lo), sl(rest, w.shape[axis]), sl(lo, lo + DECAY_LORA), z,
         sl(lo + DECAY_LORA, lo + DECAY_LORA + ICLR_LORA), z,
         sl(lo + DECAY_LORA + ICLR_LORA, rest)], axis=axis)


def _pad_rows(w):
    return jnp.concatenate([w, jnp.zeros((LORA_PAD - w.shape[0], w.shape[1]), w.dtype)], axis=0)


def _layer(x, bsz, seq, layer, w_in, w_out, attn_norm, ffn_norm, mu, w0, w_up, a0, a_up, g_up, k_k, k_a, r_k,
           gn_g, gn_b, conv_w, conv_b, ln_g, ln_b, q_norm, k_norm, ffn_up, ffn_conv, ffn_down, tm_mm=1024):
    h = _rmsnorm(x, attn_norm)
    p = _matmul_nt(h, w_in, layer, tm=tm_mm, name="proj_in")
    mu_p, mu_l = mu[:3 * RWKV_W], mu[COL_LORA:]
    y_a = _rwkv_mix(p, mu_p, mu_l, w0, w_up, a0, a_up, g_up, k_k, k_a, r_k.reshape(-1), gn_g, gn_b,
                    bsz, seq, tblk=min(SCAN_ROWS, seq))
    y_b = _conformer(p, conv_w, conv_b, ln_g, ln_b, seq)
    y_c = _stick_breaking(p, q_norm, k_norm, bsz, seq)
    x = _matmul([y_a, y_b, y_c], w_out, layer, residual=x, tm=tm_mm, name="proj_out")
    h = _rmsnorm(x, ffn_norm)
    act = _ffn_up(h, ffn_up, layer, ffn_conv, seq, tm=tm_mm)
    return _matmul([act], ffn_down, layer, residual=x, tm=tm_mm, tn=256, name="ffn_down")


def kernel(x, w_in, w_out, attn_norm, ffn_norm, rwkv_mu, rwkv_w0, rwkv_w_up, rwkv_a0, rwkv_a_up,
           rwkv_g_up, rwkv_k_k, rwkv_k_a, rwkv_r_k, rwkv_gn_g, rwkv_gn_b, conv_w, conv_b, conv_ln_g,
           conv_ln_b, sb_q_norm, sb_k_norm, ffn_up, ffn_conv, ffn_down):
    bsz, seq, d = x.shape
    x = x.reshape(bsz * seq, d)
    w_in_b = _relayout_cols(jnp.swapaxes(w_in, 1, 2).astype(BF16), 1)
    w_out_b, ffn_up_b, ffn_down_b = w_out.astype(BF16), ffn_up.astype(BF16), ffn_down.astype(BF16)
    for l in range(w_in.shape[0]):
        mu_full = jnp.concatenate([rwkv_mu[l], jnp.zeros((w_in.shape[2] - rwkv_mu.shape[1],), F32)])
        mu_re = _relayout_cols(mu_full, 0)
        x = _layer(
            x, bsz, seq, l, w_in_b, w_out_b, attn_norm[l], ffn_norm[l],
            mu_re, rwkv_w0[l], _pad_rows(rwkv_w_up[l]).astype(BF16), rwkv_a0[l],
            _pad_rows(rwkv_a_up[l]).astype(BF16), rwkv_g_up[l].astype(BF16), rwkv_k_k[l], rwkv_k_a[l],
            rwkv_r_k[l], rwkv_gn_g[l], rwkv_gn_b[l], conv_w[l], conv_b[l], conv_ln_g[l], conv_ln_b[l],
            sb_q_norm[l], sb_k_norm[l], ffn_up_b, ffn_conv[l], ffn_down_b,
            tm_mm=min(1024, seq))
    return x.reshape(bsz, seq, d)
```

```python
import functools

import jax
import jax.numpy as jnp
from jax import lax
from jax.experimental import pallas as pl
from jax.experimental.pallas import tpu as pltpu

F32 = jnp.float32
BF16 = jnp.bfloat16

D_MODEL = 4096
RWKV_HEAD = 64
RWKV_W = 2048
DECAY_LORA = 96
ICLR_LORA = 96
GATE_LORA = 256
CONV_W = 1024
CONV_K = 31
SB_HEAD = 128
SB_W = 1024
SB_HEADS = SB_W // SB_HEAD
D_FF = 2 * D_MODEL
FFN_CONV_K = 3
NORM_EPS = 1e-6
LN_EPS = 1e-5
GN_EPS = RWKV_HEAD * 1e-5

SUBLANES = 8
LORA_PAD = 128
COL_CVAL = 3 * RWKV_W
COL_CGATE = COL_CVAL + CONV_W
COL_SBQ = COL_CGATE + CONV_W
COL_SBK = COL_SBQ + SB_W
COL_SBV = COL_SBK + SB_W
COL_LORA = COL_SBV + SB_W
LORA_COLS = 2 * LORA_PAD + GATE_LORA
PROJ_PAD = COL_LORA + LORA_COLS
VMEM_LIMIT = 56 * 1024 * 1024

RWKV_C = 64
PAIR = 2 * RWKV_HEAD
PREP_COLS = 512


def _cparams(sem):
    return pltpu.CompilerParams(dimension_semantics=sem, vmem_limit_bytes=VMEM_LIMIT)


def _dot(a, b):
    return jnp.dot(a.astype(BF16), b.astype(BF16), preferred_element_type=F32)


def _dot_nt(a, b):
    return lax.dot_general(a.astype(BF16), b.astype(BF16), (((1,), (1,)), ((), ())),
                           preferred_element_type=F32)


def _dot_tn(a, b):
    return lax.dot_general(a.astype(BF16), b.astype(BF16), (((0,), (0,)), ((), ())),
                           preferred_element_type=F32)


def _split3(x):
    hi = x.astype(BF16)
    r1 = x - hi.astype(F32)
    mid = r1.astype(BF16)
    lo = (r1 - mid.astype(F32)).astype(BF16)
    return hi, mid, lo


def _dot_x3(a, b_exact):
    hi, mid, lo = _split3(a)
    b = b_exact.astype(BF16)
    return (jnp.dot(hi, b, preferred_element_type=F32)
            + jnp.dot(mid, b, preferred_element_type=F32)
            + jnp.dot(lo, b, preferred_element_type=F32))


def _x3_dot(a_exact, b):
    hi, mid, lo = _split3(b)
    a = a_exact.astype(BF16)
    return (jnp.dot(a, hi, preferred_element_type=F32)
            + jnp.dot(a, mid, preferred_element_type=F32)
            + jnp.dot(a, lo, preferred_element_type=F32))


def _dot_x2(a, b_exact):
    hi = a.astype(BF16)
    lo = (a - hi.astype(F32)).astype(BF16)
    b = b_exact.astype(BF16)
    return jnp.dot(hi, b, preferred_element_type=F32) + jnp.dot(lo, b, preferred_element_type=F32)


def _neg_abs(x):
    bits = lax.bitcast_convert_type(x, jnp.int32) | jnp.int32(-2 ** 31)
    return lax.bitcast_convert_type(bits, F32)


def _softplus(x):
    return jnp.maximum(x, 0.0) + jnp.log(1.0 + jnp.exp(_neg_abs(x)))


def _rmsnorm_kernel(x_ref, g_ref, o_ref):
    x = x_ref[...]
    ms = jnp.mean(x * x, axis=-1, keepdims=True)
    o_ref[...] = (x * lax.rsqrt(ms + NORM_EPS) * g_ref[...]).astype(o_ref.dtype)


def _rmsnorm(x, g, tm=512):
    m, d = x.shape
    return pl.pallas_call(
        _rmsnorm_kernel,
        grid=(m // tm,),
        in_specs=[pl.BlockSpec((tm, d), lambda i: (i, 0)),
                  pl.BlockSpec((1, d), lambda i: (0, 0))],
        out_specs=pl.BlockSpec((tm, d), lambda i: (i, 0)),
        out_shape=jax.ShapeDtypeStruct((m, d), BF16),
        compiler_params=_cparams(("parallel",)),
        name="rmsnorm",
    )(x, g.reshape(1, d))


def _mm_kernel(*refs, n_lhs, has_res):
    o_ref = refs[-1]
    acc = None
    for a_ref, w_ref in zip(refs[:n_lhs], refs[n_lhs:2 * n_lhs]):
        d = jnp.dot(a_ref[...], w_ref[...], preferred_element_type=F32)
        acc = d if acc is None else acc + d
    if has_res:
        acc = acc + refs[2 * n_lhs][...]
    o_ref[...] = acc.astype(o_ref.dtype)


def _mm_nt_kernel(a_ref, w_ref, o_ref):
    o_ref[...] = lax.dot_general(a_ref[...], w_ref[...], (((1,), (1,)), ((), ())),
                                 preferred_element_type=F32)


def _matmul_nt(a, w_t, layer, tm=1024, tn=512, name="matmul_nt"):
    m, k = a.shape
    n = w_t.shape[1]
    return pl.pallas_call(
        _mm_nt_kernel,
        grid=(m // tm, n // tn),
        in_specs=[pl.BlockSpec((tm, k), lambda i, j: (i, 0)),
                  pl.BlockSpec((pl.Squeezed(), tn, k), lambda i, j: (layer, j, 0))],
        out_specs=pl.BlockSpec((tm, tn), lambda i, j: (i, j)),
        out_shape=jax.ShapeDtypeStruct((m, n), F32),
        compiler_params=_cparams(("parallel", "arbitrary")),
        name=name,
    )(a, w_t)


def _matmul(lhs_list, w, layer, residual=None, tm=1024, tn=512, name="matmul"):
    m = lhs_list[0].shape[0]
    n = w.shape[2]
    in_specs, args = [], []
    for a in lhs_list:
        in_specs.append(pl.BlockSpec((tm, a.shape[1]), lambda i, j: (i, 0)))
        args.append(a)
    row = 0
    for a in lhs_list:
        k = a.shape[1]
        assert row % k == 0
        in_specs.append(pl.BlockSpec((pl.Squeezed(), k, tn),
                                     functools.partial(lambda i, j, rb: (layer, rb, j), rb=row // k)))
        args.append(w)
        row += k
    assert row == w.shape[1]
    if residual is not None:
        in_specs.append(pl.BlockSpec((tm, tn), lambda i, j: (i, j)))
        args.append(residual)
    return pl.pallas_call(
        functools.partial(_mm_kernel, n_lhs=len(lhs_list), has_res=residual is not None),
        grid=(m // tm, n // tn),
        in_specs=in_specs,
        out_specs=pl.BlockSpec((tm, tn), lambda i, j: (i, j)),
        out_shape=jax.ShapeDtypeStruct((m, n), F32),
        compiler_params=_cparams(("parallel", "arbitrary")),
        name=name,
    )(*args)


FFN_CARRY = 8
FFN_SUB = 512


def _ffn_up_kernel(h_ref, wg_ref, wv_ref, cg_ref, cv_ref, o_ref, *scratch, tm, tn, seq):
    i = pl.program_id(1)
    first = (i * tm) % seq == 0
    nsub = tn // FFN_SUB
    sg, sv = scratch[:nsub], scratch[nsub:]

    @pl.when(first)
    def _():
        for s in scratch:
            s[0:FFN_CARRY, :] = jnp.zeros((FFN_CARRY, FFN_SUB), F32)

    @pl.when(jnp.logical_not(first))
    def _():
        for s in scratch:
            s[0:FFN_CARRY, :] = s[tm:tm + FFN_CARRY, :]

    def up(w_ref, s, n):
        s[FFN_CARRY:FFN_CARRY + tm, :] = jnp.dot(h_ref[...], w_ref[:, n * FFN_SUB:(n + 1) * FFN_SUB],
                                                 preferred_element_type=F32)

    def conv(c_ref, s, n):
        acc = None
        for k in range(FFN_CONV_K):
            off = FFN_CARRY - (FFN_CONV_K - 1) + k
            term = c_ref[k:k + 1, n * FFN_SUB:(n + 1) * FFN_SUB] * s[off:off + tm, :]
            acc = term if acc is None else acc + term
        return acc

    def gate(n):
        g = conv(cg_ref, sg[n], n)
        v = conv(cv_ref, sv[n], n)
        silu = 0.5 * g * (1.0 + jnp.tanh(0.5 * g))
        o_ref[:, n * FFN_SUB:(n + 1) * FFN_SUB] = (silu * v).astype(o_ref.dtype)

    up(wg_ref, sg[0], 0)
    up(wv_ref, sv[0], 0)
    for n in range(1, nsub):
        up(wg_ref, sg[n], n)
        gate(n - 1)
        up(wv_ref, sv[n], n)
    gate(nsub - 1)


def _ffn_up(h, w_up, layer, conv, seq, tm=1024, tn=512):
    m, d = h.shape
    nj = D_FF // tn
    return pl.pallas_call(
        functools.partial(_ffn_up_kernel, tm=tm, tn=tn, seq=seq),
        grid=(nj, m // tm),
        in_specs=[
            pl.BlockSpec((tm, d), lambda j, i: (i, 0)),
            pl.BlockSpec((pl.Squeezed(), d, tn), lambda j, i: (layer, 0, j)),
            pl.BlockSpec((pl.Squeezed(), d, tn), lambda j, i: (layer, 0, j + nj)),
            pl.BlockSpec((FFN_CONV_K, tn), lambda j, i: (0, j)),
            pl.BlockSpec((FFN_CONV_K, tn), lambda j, i: (0, j + nj)),
        ],
        out_specs=pl.BlockSpec((tm, tn), lambda j, i: (i, j)),
        out_shape=jax.ShapeDtypeStruct((m, D_FF), BF16),
        scratch_shapes=[pltpu.VMEM((FFN_CARRY + tm, FFN_SUB), F32)] * (2 * (tn // FFN_SUB)),
        compiler_params=_cparams(("parallel", "arbitrary")),
        name="ffn_up_conv_gate",
    )(h, w_up, w_up, conv, conv)


CONF_HALO = 32
CONF_ROWS = 32
CONF_COLS = 256


def _conformer_kernel(val_ref, gate_ref, hval_ref, hgate_ref, cw_ref, cb_ref, lng_ref, lnb_ref,
                      o_ref, glu_s, u_s, *, tm, seq):
    i = pl.program_id(0)
    first = (i * tm) % seq == 0
    hglu = hval_ref[...] * jax.nn.sigmoid(hgate_ref[...])
    glu_s[0, 0:CONF_HALO, :] = jnp.where(first, jnp.zeros_like(hglu), hglu)
    glu_s[0, CONF_HALO:CONF_HALO + tm, :] = val_ref[...] * jax.nn.sigmoid(gate_ref[...])
    for b in range(1, SUBLANES):
        glu_s[b, SUBLANES:CONF_HALO + tm, :] = glu_s[0, SUBLANES - b:CONF_HALO + tm - b, :]
    for r0 in range(0, tm, CONF_ROWS):
        for c0 in range(0, CONV_W, CONF_COLS):
            acc = jnp.zeros((CONF_ROWS, CONF_COLS), F32) + cb_ref[:, c0:c0 + CONF_COLS]
            for k in range(CONV_K):
                a, b = divmod(CONV_K - 1 - k, SUBLANES)
                start = CONF_HALO + r0 - SUBLANES * a
                acc = acc + cw_ref[k:k + 1, c0:c0 + CONF_COLS] * glu_s[b, start:start + CONF_ROWS,
                                                                   c0:c0 + CONF_COLS]
            u_s[r0:r0 + CONF_ROWS, c0:c0 + CONF_COLS] = acc
    u = u_s[...]
    mean = jnp.mean(u, axis=-1, keepdims=True)
    d = u - mean
    var = jnp.mean(d * d, axis=-1, keepdims=True)
    y = d * lax.rsqrt(var + LN_EPS) * lng_ref[...] + lnb_ref[...]
    o_ref[...] = (y * jax.nn.sigmoid(y)).astype(o_ref.dtype)


def _conformer(p, cw, cb, lng, lnb, seq, tm=256):
    m = p.shape[0]
    vb, gb = COL_CVAL // CONV_W, COL_CGATE // CONV_W
    hmap = lambda i: jnp.maximum(i * (tm // CONF_HALO) - 1, 0)
    return pl.pallas_call(
        functools.partial(_conformer_kernel, tm=tm, seq=seq),
        grid=(m // tm,),
        in_specs=[
            pl.BlockSpec((tm, CONV_W), lambda i: (i, vb)),
            pl.BlockSpec((tm, CONV_W), lambda i: (i, gb)),
            pl.BlockSpec((CONF_HALO, CONV_W), lambda i: (hmap(i), vb)),
            pl.BlockSpec((CONF_HALO, CONV_W), lambda i: (hmap(i), gb)),
            pl.BlockSpec((CONV_K, CONV_W), lambda i: (0, 0)),
            pl.BlockSpec((1, CONV_W), lambda i: (0, 0)),
            pl.BlockSpec((1, CONV_W), lambda i: (0, 0)),
            pl.BlockSpec((1, CONV_W), lambda i: (0, 0)),
        ],
        out_specs=pl.BlockSpec((tm, CONV_W), lambda i: (i, 0)),
        out_shape=jax.ShapeDtypeStruct((m, CONV_W), BF16),
        scratch_shapes=[pltpu.VMEM((SUBLANES, CONF_HALO + tm, CONV_W), F32),
                        pltpu.VMEM((tm, CONV_W), F32)],
        compiler_params=_cparams(("parallel",)),
        name="conformer_conv",
    )(p, p, p, p, cw, cb.reshape(1, -1), lng.reshape(1, -1), lnb.reshape(1, -1))


SB_BQ = 1024
SB_BK = 256
SB_PREP_ROWS = 512
SB_GROUP = 8
LOG2E = 1.4426950408889634
LN2 = 0.6931471805599453


def _sb_kernel(q_ref, k_ref, v_ref, qg_ref, kg_ref, o_ref, kn_s, vb_s, *, seq, bq):
    qi = pl.program_id(2)

    @pl.when(qi == 0)
    def _():
        def body(c, carry):
            sl = pl.ds(pl.multiple_of(c * SB_PREP_ROWS, SB_PREP_ROWS), SB_PREP_ROWS)
            k = k_ref[sl, :]
            ms = jnp.mean(k * k, axis=-1, keepdims=True)
            kn_s[sl, :] = (k * lax.rsqrt(ms + NORM_EPS) * kg_ref[...]).astype(BF16)
            vb_s[sl, :] = v_ref[sl, :].astype(BF16)
            return carry
        lax.fori_loop(0, seq // SB_PREP_ROWS, body, 0)

    q = q_ref[...]
    ms = jnp.mean(q * q, axis=-1, keepdims=True)
    q = (q * lax.rsqrt(ms + NORM_EPS) * qg_ref[...] * (SB_HEAD ** -0.5 * LOG2E)).astype(BF16)

    row = lax.broadcasted_iota(jnp.int32, (SB_BK, SB_BK), 0)
    col = lax.broadcasted_iota(jnp.int32, (SB_BK, SB_BK), 1)
    later = (row > col).astype(BF16)
    causal = col < row

    def blocks(qs, k0s, nsub, cs, accs, masked):
        n = range(len(qs))
        sls = [pl.ds(pl.multiple_of(k0, SB_BK), nsub * SB_BK) for k0 in k0s]
        zs = [lax.dot_general(qs[i], kn_s[sls[i], :], (((1,), (1,)), ((), ())), preferred_element_type=F32)
              for i in n]
        sps = [jnp.maximum(z, 0.0) + jnp.log(1.0 + jnp.exp2(_neg_abs(z))) * (1.0 / LN2) for z in zs]
        if masked:
            sps = [jnp.where(causal, sp, 0.0) for sp in sps]
        atts = [[None] * nsub for _ in n]
        for j in reversed(range(nsub)):
            cols = slice(j * SB_BK, (j + 1) * SB_BK)
            sp_j = [sp[:, cols] for sp in sps]
            suffix = [_dot(x, later) for x in sp_j]
            att = [jnp.exp2((zs[i][:, cols] - sp_j[i]) - (suffix[i] + cs[i])) for i in n]
            if masked:
                att = [jnp.where(causal, x, 0.0) for x in att]
            for i in n:
                atts[i][j] = att[i].astype(BF16)
            cs = [cs[i] + jnp.sum(sp_j[i], axis=-1, keepdims=True) for i in n]
        accs = [accs[i] + jnp.dot(atts[i][0] if nsub == 1 else jnp.concatenate(atts[i], axis=1),
                                  vb_s[sls[i], :], preferred_element_type=F32) for i in n]
        return cs, accs

    ntile = bq // SB_BK
    tiles = range(ntile)
    qs = [q[h * SB_BK:(h + 1) * SB_BK, :] for h in tiles]
    cs = [jnp.zeros((SB_BK, 1), F32) for _ in tiles]
    accs = [jnp.zeros((SB_BK, SB_HEAD), F32) for _ in tiles]
    base = qi * bq
    cs, accs = blocks(qs, [base + h * SB_BK for h in tiles], 1, cs, accs, True)
    for s in range(1, ntile):
        sub = list(range(s, ntile))
        c1, a1 = blocks([qs[h] for h in sub], [base + (h - s) * SB_BK for h in sub], 1,
                        [cs[h] for h in sub], [accs[h] for h in sub], False)
        cs, accs = cs[:s] + c1, accs[:s] + a1

    def run(nsub, start):
        def body(n, carry):
            k0 = start - (n + 1) * nsub * SB_BK
            c, a = blocks(qs, [k0] * ntile, nsub, list(carry[0]), list(carry[1]), False)
            return tuple(c), tuple(a)
        return body

    before = qi * ntile
    rem = (before % SB_GROUP) // ntile
    carry = lax.fori_loop(0, rem, run(ntile, base), (tuple(cs), tuple(accs)))
    cs, accs = lax.fori_loop(0, before // SB_GROUP, run(SB_GROUP, base - rem * ntile * SB_BK), carry)
    for h in tiles:
        o_ref[h * SB_BK:(h + 1) * SB_BK, :] = accs[h].astype(o_ref.dtype)


def _stick_breaking(p, qg, kg, bsz, seq):
    bq = min(SB_BQ, seq)
    assert bq % SB_BK == 0 and SB_GROUP % (bq // SB_BK) == 0
    nq = seq // bq
    qb, kb, vb = COL_SBQ // SB_HEAD, COL_SBK // SB_HEAD, COL_SBV // SB_HEAD
    return pl.pallas_call(
        functools.partial(_sb_kernel, seq=seq, bq=bq),
        grid=(bsz, SB_HEADS, nq),
        in_specs=[
            pl.BlockSpec((bq, SB_HEAD), lambda b, h, q: (b * nq + q, qb + h)),
            pl.BlockSpec((seq, SB_HEAD), lambda b, h, q: (b, kb + h)),
            pl.BlockSpec((seq, SB_HEAD), lambda b, h, q: (b, vb + h)),
            pl.BlockSpec((1, SB_HEAD), lambda b, h, q: (0, 0)),
            pl.BlockSpec((1, SB_HEAD), lambda b, h, q: (0, 0)),
        ],
        out_specs=pl.BlockSpec((bq, SB_HEAD), lambda b, h, q: (b * nq + q, h)),
        out_shape=jax.ShapeDtypeStruct((bsz * seq, SB_W), BF16),
        scratch_shapes=[pltpu.VMEM((seq, SB_HEAD), BF16), pltpu.VMEM((seq, SB_HEAD), BF16)],
        compiler_params=_cparams(("parallel", "parallel", "arbitrary")),
        name="stick_breaking",
    )(p, p, p, qg.reshape(1, -1), kg.reshape(1, -1))


PREP_HALO = 8


def _head_ones():
    r = lax.broadcasted_iota(jnp.int32, (PAIR, PAIR), 0) // RWKV_HEAD
    c = lax.broadcasted_iota(jnp.int32, (PAIR, PAIR), 1) // RWKV_HEAD
    return (r == c).astype(BF16)


def _rwkv_prep_kernel(p_ref, lora_ref, hp_ref, hlora_ref, mu_ref, mul_ref, w0_ref, wup_ref, a0_ref,
                      aup_ref, gup_ref, kk_ref, ka_ref, rk_ref,
                      r_o, k_o, v_o, lw_o, a_o, b_o, bonus_o, g_o, *, first):
    def shifted(x_ref, h_ref, m_ref, c0, c1):
        x = x_ref[:, c0:c1]
        prev = pltpu.roll(x, 1, 0)
        row = lax.broadcasted_iota(jnp.int32, x.shape, 0)
        last = h_ref[PREP_HALO - 1:PREP_HALO, c0:c1]
        last = jnp.where(first, jnp.zeros_like(last), last)
        prev = jnp.where(row == 0, last, prev)
        return x + m_ref[:, c0:c1] * (prev - x)

    lo = shifted(lora_ref, hlora_ref, mul_ref, 0, LORA_COLS)
    w_act = jnp.tanh(lo[:, 0:LORA_PAD]).astype(BF16)
    a_act = lo[:, LORA_PAD:2 * LORA_PAD].astype(BF16)
    g_act = jax.nn.sigmoid(lo[:, 2 * LORA_PAD:]).astype(BF16)
    ones = _head_ones()

    for c0 in range(0, RWKV_W, PREP_COLS):
        c1 = c0 + PREP_COLS
        r = shifted(p_ref, hp_ref, mu_ref, c0, c1)
        k = shifted(p_ref, hp_ref, mu_ref, RWKV_W + c0, RWKV_W + c1)
        v = shifted(p_ref, hp_ref, mu_ref, 2 * RWKV_W + c0, 2 * RWKV_W + c1)
        y = w0_ref[:, c0:c1] + _dot(w_act, wup_ref[:, c0:c1])
        log_w = -_softplus(-y) - 0.5
        lw_o[:, c0:c1] = -jnp.exp(log_w)
        iclr = jax.nn.sigmoid(a0_ref[:, c0:c1] + _dot(a_act, aup_ref[:, c0:c1]))
        g_o[:, c0:c1] = _dot(g_act, gup_ref[:, c0:c1])
        kk = k * kk_ref[:, c0:c1]
        kt = k * (1.0 + (iclr - 1.0) * ka_ref[:, c0:c1])
        rkr = r * kt * rk_ref[:, c0:c1]
        for t0 in range(0, PREP_COLS, PAIR):
            t1 = t0 + PAIR
            kk_t = kk[:, t0:t1]
            ss = _dot_x2(kk_t * kk_t, ones)
            kk_n = kk_t * lax.rsqrt(jnp.maximum(ss, 1e-24))
            a_o[:, c0 + t0:c0 + t1] = -kk_n
            b_o[:, c0 + t0:c0 + t1] = kk_n * iclr[:, t0:t1]
            bonus_o[:, c0 + t0:c0 + t1] = _dot(rkr[:, t0:t1], ones) * v[:, t0:t1]
        r_o[:, c0:c1] = r
        k_o[:, c0:c1] = kt
        v_o[:, c0:c1] = v


GN_ROWS = 256


QUAD_HEADS = 2
QUAD = QUAD_HEADS * RWKV_HEAD
SCAN_GROUPS = 16
SCAN_ROWS = 256


def _rwkv_quad_kernel(r_ref, k_ref, v_ref, lw_ref, a_ref, b_ref, bonus_ref, g_ref, gng_ref, gnb_ref,
                      o_ref, s_ref, rq_s, oin_s, tr_s, dl_s, *, tblk, t_id):
    C = RWKV_C
    HC = QUAD_HEADS * C
    G = range(SCAN_GROUPS)

    @pl.when(t_id == 0)
    def _():
        s_ref[...] = jnp.zeros_like(s_ref)

    lane = lax.broadcasted_iota(jnp.int32, (1, QUAD), 1) // RWKV_HEAD
    hm = [(lane == h).astype(F32) for h in range(QUAD_HEADS)]
    ti = lax.broadcasted_iota(jnp.int32, (C, C), 0)
    ii = lax.broadcasted_iota(jnp.int32, (C, C), 1)
    tri_incl = (ii <= ti).astype(BF16)
    t4 = lax.broadcasted_iota(jnp.int32, (C, HC), 0)
    i4 = lax.broadcasted_iota(jnp.int32, (C, HC), 1) % C
    strict4 = (i4 < t4).astype(F32)
    incl4 = (i4 <= t4).astype(F32)
    mask_ak_qk = jnp.concatenate([strict4, incl4], axis=0)
    rr = lax.broadcasted_iota(jnp.int32, (QUAD, QUAD), 0)
    cc = lax.broadcasted_iota(jnp.int32, (QUAD, QUAD), 1)
    bdmask = (rr // RWKV_HEAD == cc // RWKV_HEAD).astype(F32)
    eye = (rr == cc).astype(F32)
    ones = (rr // RWKV_HEAD == cc // RWKV_HEAD).astype(BF16)

    def stack_heads(x):
        return jnp.concatenate([x * m for m in hm], axis=0)

    def stack_heads2(x, y):
        return jnp.concatenate([jnp.concatenate([x * m, y * m], axis=1) for m in hm], axis=0)

    def chunk(c, carry):
        sl = pl.ds(pl.multiple_of(c * C, C), C)
        lns = [slice(g * QUAD, (g + 1) * QUAD) for g in G]
        lw = [lw_ref[sl, ln] for ln in lns]
        cum = [_x3_dot(tri_incl, x) for x in lw]
        tot = [x[C - 1:C, :] for x in cum]
        e_neg = [jnp.exp(-x) for x in cum]
        e_end = [jnp.exp(t - x) for t, x in zip(tot, cum)]
        at = [a_ref[sl, ln] * jnp.exp(x - w) for ln, x, w in zip(lns, cum, lw)]
        rt = [r_ref[sl, ln] * jnp.exp(x) for ln, x in zip(lns, cum)]
        b = [b_ref[sl, ln] for ln in lns]
        k = [k_ref[sl, ln] for ln in lns]
        v = [v_ref[sl, ln] for ln in lns]
        gram = [_dot_nt(jnp.concatenate([at[g], rt[g]], axis=0),
                        jnp.concatenate([stack_heads(b[g] * e_neg[g]), stack_heads(k[g] * e_neg[g])], axis=0))
                for g in G]
        q_b = [x[C:, 0:HC] * incl4 for x in gram]
        vv = [stack_heads(x) for x in v]
        mv = [_dot(x[:, HC:] * mask_ak_qk, vv[g]) for g, x in enumerate(gram)]
        u_rhs = [x[0:C, :] for x in mv]
        o_in2 = [x[C:, :] for x in mv]
        pw = [stack_heads(x[0:C, 0:HC] * strict4) for x in gram]
        t_inv = [eye + x for x in pw]
        pw = [_dot(x, x) for x in pw]
        for _ in range(4):
            both = [_dot(jnp.concatenate([x, t], axis=0), x) for t, x in zip(t_inv, pw)]
            pw = [x[0:HC, :] for x in both]
            t_inv = [t + x[HC:, :] for t, x in zip(t_inv, both)]
        t_inv = [t + _dot(t, x) for t, x in zip(t_inv, pw)]
        t_ls = [sum(t[h * C:(h + 1) * C, :] for h in range(1, QUAD_HEADS)) + t[0:C, :] for t in t_inv]
        sol = [_dot(t_ls[g], stack_heads2(at[g], u_rhs[g])) for g in G]
        wa = [x[:, 0:QUAD] for x in sol]
        u0 = [x[:, QUAD:] for x in sol]
        y = [_dot(q_b[g], stack_heads2(wa[g], u0[g])) for g in G]
        td = [_dot_tn(jnp.concatenate([sol[g], jnp.concatenate([jnp.zeros_like(v[g]), v[g]], axis=1)], axis=0),
                      jnp.concatenate([b[g] * e_end[g], k[g] * e_end[g]], axis=0)) for g in G]
        for g in G:
            rq_s[sl, lns[g]] = rt[g] + y[g][:, 0:QUAD]
            oin_s[sl, lns[g]] = y[g][:, QUAD:] + o_in2[g]
            tr_s[g, c] = td[g][0:QUAD, :] * bdmask + eye * jnp.exp(tot[g])
            dl_s[g, c] = td[g][QUAD:, :] * bdmask
        return carry

    lax.fori_loop(0, tblk // C, chunk, 0)

    def scan(c, carry):
        sl = pl.ds(pl.multiple_of(c * C, C), C)
        for g in G:
            ln = slice(g * QUAD, (g + 1) * QUAD)
            s = s_ref[g]
            oin_s[sl, ln] = _dot_nt(rq_s[sl, ln], s) + oin_s[sl, ln]
            s_ref[g] = _dot(s, tr_s[g, c]) + dl_s[g, c]
        return carry

    lax.fori_loop(0, tblk // C, scan, 0)

    for g in G:
        ln = slice(g * QUAD, (g + 1) * QUAD)
        for r0 in range(0, tblk, GN_ROWS):
            rows = slice(r0, r0 + GN_ROWS)
            o = oin_s[rows, ln]
            mean = _dot_x3(o, ones) * (1.0 / RWKV_HEAD)
            d = o - mean
            var = _dot_x2(d * d, ones) * (1.0 / RWKV_HEAD)
            on = d * lax.rsqrt(var + GN_EPS) * gng_ref[:, ln] + gnb_ref[:, ln]
            o_ref[rows, ln] = ((on + bonus_ref[rows, ln]) * g_ref[rows, ln]).astype(o_ref.dtype)


def _rwkv_fused_kernel(p_ref, lora_ref, hp_ref, hlora_ref, mu_ref, mul_ref, w0_ref, wup_ref, a0_ref,
                       aup_ref, gup_ref, kk_ref, ka_ref, rk_ref, gng_ref, gnb_ref, o_ref,
                       r_s, k_s, v_s, lw_s, a_s, b_s, bonus_s, g_s, s_ref, rq_s, oin_s, tr_s, dl_s, *, tblk):
    t_id = pl.program_id(1)
    _rwkv_prep_kernel(p_ref, lora_ref, hp_ref, hlora_ref, mu_ref, mul_ref, w0_ref, wup_ref, a0_ref,
                      aup_ref, gup_ref, kk_ref, ka_ref, rk_ref,
                      r_s, k_s, v_s, lw_s, a_s, b_s, bonus_s, g_s, first=t_id == 0)
    _rwkv_quad_kernel(r_s, k_s, v_s, lw_s, a_s, b_s, bonus_s, g_s, gng_ref, gnb_ref,
                      o_ref, s_ref, rq_s, oin_s, tr_s, dl_s, tblk=tblk, t_id=t_id)


def _rwkv_mix(p, mu, mul, w0, wup, a0, aup, gup, kk, ka, rk, gng, gnb, bsz, seq, tblk):
    assert SCAN_GROUPS * QUAD == RWKV_W and RWKV_C == RWKV_HEAD
    nt = seq // tblk
    nch = tblk // RWKV_C
    lb = COL_LORA // LORA_COLS
    blk = lambda b, t: b * nt + t
    hmap = lambda b, t: jnp.maximum(blk(b, t) * (tblk // PREP_HALO) - 1, 0)
    row = lambda a: a.reshape(1, -1)
    vec = pl.BlockSpec((1, RWKV_W), lambda b, t: (0, 0))
    full = lambda r, c: pl.BlockSpec((r, c), lambda b, t: (0, 0))
    act = pltpu.VMEM((tblk, RWKV_W), F32)
    return pl.pallas_call(
        functools.partial(_rwkv_fused_kernel, tblk=tblk),
        grid=(bsz, nt),
        in_specs=[
            pl.BlockSpec((tblk, 3 * RWKV_W), lambda b, t: (blk(b, t), 0)),
            pl.BlockSpec((tblk, LORA_COLS), lambda b, t: (blk(b, t), lb)),
            pl.BlockSpec((PREP_HALO, 3 * RWKV_W), lambda b, t: (hmap(b, t), 0)),
            pl.BlockSpec((PREP_HALO, LORA_COLS), lambda b, t: (hmap(b, t), lb)),
            full(1, 3 * RWKV_W), full(1, LORA_COLS),
            vec, full(LORA_PAD, RWKV_W), vec, full(LORA_PAD, RWKV_W), full(GATE_LORA, RWKV_W),
            vec, vec, vec, vec, vec,
        ],
        out_specs=pl.BlockSpec((tblk, RWKV_W), lambda b, t: (blk(b, t), 0)),
        out_shape=jax.ShapeDtypeStruct((bsz * seq, RWKV_W), BF16),
        scratch_shapes=[act] * 8 + [
            pltpu.VMEM((SCAN_GROUPS, QUAD, QUAD), F32), act, act,
            pltpu.VMEM((SCAN_GROUPS, nch, QUAD, QUAD), F32),
            pltpu.VMEM((SCAN_GROUPS, nch, QUAD, QUAD), F32)],
        compiler_params=_cparams(("parallel", "arbitrary")),
        name="rwkv_mix",
    )(p, p, p, p, row(mu), row(mul), row(w0), wup, row(a0), aup, gup, row(kk), row(ka), row(rk),
      row(gng), row(gnb))


def _relayout_cols(w, axis):
    def sl(a, b):
        idx = [slice(None)] * w.ndim
        idx[axis] = slice(a, b)
        return w[tuple(idx)]
    pad_shape = list(w.shape)
    pad_shape[axis] = LORA_PAD - DECAY_LORA
    z = jnp.zeros(pad_shape, w.dtype)
    lo = 3 * RWKV_W
    rest = lo + DECAY_LORA + ICLR_LORA + GATE_LORA
    return jnp.concatenate(
        [sl(0, lo), sl(rest, w.shape[axis]), sl(lo, lo + DECAY_LORA), z,
         sl(lo + DECAY_LORA, lo + DECAY_LORA + ICLR_LORA), z,
         sl(lo + DECAY_LORA + ICLR_LORA, rest)], axis=axis)


def _pad_rows(w):
    return jnp.concatenate([w, jnp.zeros((LORA_PAD - w.shape[0], w.shape[1]), w.dtype)], axis=0)


def _layer(x, bsz, seq, layer, w_in, w_out, attn_norm, ffn_norm, mu, w0, w_up, a0, a_up, g_up, k_k, k_a, r_k,
           gn_g, gn_b, conv_w, conv_b, ln_g, ln_b, q_norm, k_norm, ffn_up, ffn_conv, ffn_down, tm_mm=1024):
    h = _rmsnorm(x, attn_norm)
    p = _matmul_nt(h, w_in, layer, tm=tm_mm, name="proj_in")
    mu_p, mu_l = mu[:3 * RWKV_W], mu[COL_LORA:]
    y_a = _rwkv_mix(p, mu_p, mu_l, w0, w_up, a0, a_up, g_up, k_k, k_a, r_k.reshape(-1), gn_g, gn_b,
                    bsz, seq, tblk=min(SCAN_ROWS, seq))
    y_b = _conformer(p, conv_w, conv_b, ln_g, ln_b, seq)
    y_c = _stick_breaking(p, q_norm, k_norm, bsz, seq)
    x = _matmul([y_a, y_b, y_c], w_out, layer, residual=x, tm=tm_mm, name="proj_out")
    h = _rmsnorm(x, ffn_norm)
    act = _ffn_up(h, ffn_up, layer, ffn_conv, seq, tm=tm_mm)
    return _matmul([act], ffn_down, layer, residual=x, tm=tm_mm, tn=256, name="ffn_down")


def kernel(x, w_in, w_out, attn_norm, ffn_norm, rwkv_mu, rwkv_w0, rwkv_w_up, rwkv_a0, rwkv_a_up,
           rwkv_g_up, rwkv_k_k, rwkv_k_a, rwkv_r_k, rwkv_gn_g, rwkv_gn_b, conv_w, conv_b, conv_ln_g,
           conv_ln_b, sb_q_norm, sb_k_norm, ffn_up, ffn_conv, ffn_down):
    bsz, seq, d = x.shape
    x = x.reshape(bsz * seq, d)
    w_in_b = _relayout_cols(jnp.swapaxes(w_in, 1, 2).astype(BF16), 1)
    w_out_b, ffn_up_b, ffn_down_b = w_out.astype(BF16), ffn_up.astype(BF16), ffn_down.astype(BF16)
    for l in range(w_in.shape[0]):
        mu_full = jnp.concatenate([rwkv_mu[l], jnp.zeros((w_in.shape[2] - rwkv_mu.shape[1],), F32)])
        mu_re = _relayout_cols(mu_full, 0)
        x = _layer(
            x, bsz, seq, l, w_in_b, w_out_b, attn_norm[l], ffn_norm[l],
            mu_re, rwkv_w0[l], _pad_rows(rwkv_w_up[l]).astype(BF16), rwkv_a0[l],
            _pad_rows(rwkv_a_up[l]).astype(BF16), rwkv_g_up[l].astype(BF16), rwkv_k_k[l], rwkv_k_a[l],
            rwkv_r_k[l], rwkv_gn_g[l], rwkv_gn_b[l], conv_w[l], conv_b[l], conv_ln_g[l], conv_ln_b[l],
            sb_q_norm[l], sb_k_norm[l], ffn_up_b, ffn_conv[l], ffn_down_b,
            tm_mm=min(1024, seq))
    return x.reshape(bsz, seq, d)
```
